```python
import jax, jax.numpy as jnp
from jax import lax
import numpy as np

D_MODEL = 1024
BATCH = 8
SEQ = 2048
DEPTH = 4
DEC_BATCH = 128
DEC_SEQ = 1
PAST_LEN = 2048
PAGE_SIZE = 128

HEAD_DIM = 64
A_WIDTH = D_MODEL // 4
A_GROUPS = A_WIDTH // HEAD_DIM
CHUNK = 128
B_WIDTH = D_MODEL // 2
B_HEADS = B_WIDTH // HEAD_DIM
DECAY_LORA = 64
ICLR_LORA = 64
GATE_LORA = 128
B_COLS = 3 * B_WIDTH + DECAY_LORA + ICLR_LORA + GATE_LORA
C_WIDTH = D_MODEL // 4
C_HEADS = C_WIDTH // HEAD_DIM
SB_BLOCK = 128
SB_SCALE = HEAD_DIM ** -0.5
SB_BIAS_INIT = -6.0
MIX_WIDTH = A_WIDTH + B_WIDTH + C_WIDTH
IN_COLS = 2 * A_WIDTH + B_COLS + 3 * C_WIDTH
D_FF = -(-8 * D_MODEL // (3 * 256)) * 256
PLE_DIM = 256
RMS_EPS = 1e-6
LN_EPS = 1e-5
GN_EPS = 64e-5

kernel_name = 'hybrid_gmlp_rwkv7_stickbreak_decoder_step'


def rmsnorm(x, g):
    xf = x.astype(jnp.float32)
    y = xf * lax.rsqrt(jnp.mean(xf * xf, axis=-1, keepdims=True) + RMS_EPS)
    return (y * g.astype(jnp.float32)).astype(x.dtype)


def layernorm(x, g, b):
    xf = x.astype(jnp.float32)
    xc = xf - jnp.mean(xf, axis=-1, keepdims=True)
    y = xc * lax.rsqrt(jnp.mean(xc * xc, axis=-1, keepdims=True) + LN_EPS)
    return (y * g.astype(jnp.float32) + b.astype(jnp.float32)).astype(x.dtype)


def chunk_spatial_gate(u, v, w_s, b_s):
    bsz, t = u.shape[:2]
    n_chunks = -(-t // CHUNK)
    pad = n_chunks * CHUNK - t
    vp = jnp.pad(v, ((0, 0), (0, pad), (0, 0), (0, 0))).reshape(bsz, n_chunks, CHUNK, A_GROUPS, HEAD_DIM)
    causal = jnp.tril(jnp.ones((CHUNK, CHUNK), dtype=bool))
    w = jnp.where(causal[None], w_s, 0.0)
    mixed = jnp.einsum('gij,bcjgd->bcigd', w, vp) + b_s.T[None, None, :, :, None]
    mixed = mixed.reshape(bsz, n_chunks * CHUNK, A_GROUPS, HEAD_DIM)[:, :t]
    return u * mixed


def rwkv7_mix(pb, shift_prev, wkv_prev, mu, w0, w2, a0, a2, g2, k_k, k_a, r_k, ln_g, ln_b):
    bsz, t, _ = pb.shape
    prev = jnp.concatenate([shift_prev[:, None].astype(pb.dtype), pb[:, :-1]], axis=1)
    xs = pb + (prev - pb) * mu
    r, k, v, wl, al, gl = jnp.split(
        xs, [B_WIDTH, 2 * B_WIDTH, 3 * B_WIDTH, 3 * B_WIDTH + DECAY_LORA,
             3 * B_WIDTH + DECAY_LORA + ICLR_LORA], axis=-1)
    w_log = -jax.nn.softplus(-(w0 + jnp.tanh(wl) @ w2)) - 0.5
    decay = jnp.exp(-jnp.exp(w_log.astype(jnp.float32)))
    a = jax.nn.sigmoid(a0 + al @ a2)
    g = jax.nn.sigmoid(gl) @ g2

    def heads(z):
        return z.reshape(bsz, t, B_HEADS, HEAD_DIM).astype(jnp.float32)

    kk = heads(k * k_k)
    kk = kk * lax.rsqrt(jnp.sum(kk * kk, axis=-1, keepdims=True) + 1e-12)
    k = k * (1.0 + (a - 1.0) * k_a)
    rh, kh, vh, ah, wh = heads(r), heads(k), heads(v), heads(a), heads(decay)

    def step(S, inp):
        r_t, w_t, k_t, v_t, kk_t, a_t = inp
        s_kk = jnp.einsum('bhij,bhj->bhi', S, kk_t)
        S = (S * w_t[:, :, None, :]
             - s_kk[:, :, :, None] * (kk_t * a_t)[:, :, None, :]
             + v_t[:, :, :, None] * k_t[:, :, None, :])
        return S, jnp.einsum('bhij,bhj->bhi', S, r_t)

    seq_in = tuple(jnp.moveaxis(z, 1, 0) for z in (rh, wh, kh, vh, kk, ah))
    s_fin, ys = lax.scan(step, wkv_prev.astype(jnp.float32), seq_in)
    y = jnp.moveaxis(ys, 0, 1)
    yc = y - jnp.mean(y, axis=-1, keepdims=True)
    yn = yc * lax.rsqrt(jnp.mean(yc * yc, axis=-1, keepdims=True) + GN_EPS)
    yn = yn.reshape(bsz, t, B_WIDTH) * ln_g + ln_b
    bonus = jnp.sum(rh * kh * r_k, axis=-1, keepdims=True) * vh
    out = (yn + bonus.reshape(bsz, t, B_WIDTH)) * g
    return out.astype(pb.dtype), s_fin, pb[:, -1]


def sb_attend(q, k, v, q_pos, k_pos, bias):
    s = (jnp.einsum('bqhd,bkhd->bhqk', q.astype(jnp.float32), k.astype(jnp.float32)) * SB_SCALE
         + bias.astype(jnp.float32)[None, :, None, None])
    causal = k_pos[None, :] < q_pos[:, None]
    log_beta = jax.nn.log_sigmoid(s)
    log_keep = jnp.where(causal, jax.nn.log_sigmoid(-s), 0.0)
    log_keep_after = lax.cumsum(log_keep, axis=3, reverse=True) - log_keep
    att = jnp.where(causal, jnp.exp(log_beta + log_keep_after), 0.0)
    return jnp.einsum('bhqk,bkhd->bqhd', att, v.astype(jnp.float32))


def sb_prompt(q, k, v, bias):
    bsz, t = q.shape[:2]
    n_blk = -(-t // SB_BLOCK)
    qp = jnp.pad(q, ((0, 0), (0, n_blk * SB_BLOCK - t), (0, 0), (0, 0)))
    qb = jnp.moveaxis(qp.reshape(bsz, n_blk, SB_BLOCK, C_HEADS, HEAD_DIM), 1, 0)
    k_pos = jnp.arange(t)

    def block(args):
        q_blk, b_idx = args
        q_pos = b_idx * SB_BLOCK + jnp.arange(SB_BLOCK)
        return sb_attend(q_blk, k, v, q_pos, k_pos, bias)

    out = lax.map(block, (qb, jnp.arange(n_blk)))
    return jnp.moveaxis(out, 0, 1).reshape(bsz, n_blk * SB_BLOCK, C_HEADS, HEAD_DIM)[:, :t]


def run_trunk(x, pe, shift0, wkv0, paged, P):
    h = x
    bsz, t, _ = x.shape
    k_rows, v_rows, wkv_fin, shift_fin, av_rows = [], [], [], [], []
    for i in range(DEPTH):
        hn = rmsnorm(h, P['mix_norm'][i])
        proj = hn @ P['w_in'][i]
        pa, pb, pc = jnp.split(proj, [2 * A_WIDTH, 2 * A_WIDTH + B_COLS], axis=-1)
        a_u, a_v = jnp.split(pa, 2, axis=-1)
        a_v = layernorm(a_v, P['a_ln_g'][i], P['a_ln_b'][i])
        ya = chunk_spatial_gate(a_u.reshape(bsz, t, A_GROUPS, HEAD_DIM),
                                a_v.reshape(bsz, t, A_GROUPS, HEAD_DIM),
                                P['a_ws'][i], P['a_bs'][i]).reshape(bsz, t, A_WIDTH)
        yb, s_new, sh_new = rwkv7_mix(pb, shift0[i], wkv0[i], P['b_mu'][i], P['b_w0'][i], P['b_w2'][i],
                                      P['b_a0'][i], P['b_a2'][i], P['b_g2'][i], P['b_kk'][i],
                                      P['b_ka'][i], P['b_rk'][i], P['b_ln_g'][i], P['b_ln_b'][i])
        q, k, v = (z.reshape(bsz, t, C_HEADS, HEAD_DIM) for z in jnp.split(pc, 3, axis=-1))
        q = rmsnorm(q, P['c_qn'][i])
        k = rmsnorm(k, P['c_kn'][i])
        if paged is None:
            yc = sb_prompt(q, k, v, P['c_bias'][i])
        else:
            cache_k, cache_v, page_table = paged
            past = page_table.shape[1] * cache_k.shape[2]
            k_past = cache_k[i][page_table].reshape(bsz, past, C_HEADS, HEAD_DIM).astype(k.dtype)
            v_past = cache_v[i][page_table].reshape(bsz, past, C_HEADS, HEAD_DIM).astype(v.dtype)
            k_all = jnp.concatenate([k_past, k], axis=1)
            v_all = jnp.concatenate([v_past, v], axis=1)
            yc = sb_attend(q, k_all, v_all, past + jnp.arange(t), jnp.arange(past + t), P['c_bias'][i])
            av_rows.append(a_v)
        yc = yc.astype(h.dtype).reshape(bsz, t, C_WIDTH)
        mix = jnp.concatenate([ya.astype(h.dtype), yb.astype(h.dtype), yc], axis=-1)
        h = h + mix @ P['w_out'][i]
        hn = rmsnorm(h, P['ffn_norm'][i])
        h = h + (jax.nn.silu(hn @ P['w_gate'][i]) * (hn @ P['w_up'][i])) @ P['w_down'][i]
        gate = jax.nn.sigmoid(rmsnorm(h, P['ple_norm'][i]) @ P['w_ple_gate'][i])
        h = h + (pe[i] @ P['w_ple'][i]) * gate
        k_rows.append(k)
        v_rows.append(v)
        wkv_fin.append(s_new)
        shift_fin.append(sh_new)
    av = jnp.stack(av_rows) if av_rows else None
    return h, jnp.stack(k_rows), jnp.stack(v_rows), jnp.stack(wkv_fin), jnp.stack(shift_fin), av


def setup_inputs(seed: int = 0) -> dict:
    key = jax.random.key(seed)
    ks = iter(jax.random.split(key, 64))

    def nrm(shape, scale=1.0):
        return jax.random.normal(next(ks), shape, jnp.float32) * scale

    def unif(shape, lo, hi):
        return jax.random.uniform(next(ks), shape, jnp.float32, lo, hi)

    def gain(shape):
        return 1.0 + nrm(shape, 0.02)

    n_pages = PAST_LEN // PAGE_SIZE
    n_used = DEC_BATCH * n_pages
    n_phys = n_used + max(n_used // 4, 1)
    page_table = jax.random.permutation(next(ks), n_phys)[:n_used].reshape(DEC_BATCH, n_pages).astype(jnp.int32)
    L = DEPTH
    return {
        'x_prompt': nrm((BATCH, SEQ, D_MODEL)),
        'x_sample': nrm((DEC_BATCH, DEC_SEQ, D_MODEL)),
        'cache_k': nrm((L, n_phys, PAGE_SIZE, C_HEADS, HEAD_DIM)),
        'cache_v': nrm((L, n_phys, PAGE_SIZE, C_HEADS, HEAD_DIM)),
        'state_wkv': nrm((L, DEC_BATCH, B_HEADS, HEAD_DIM, HEAD_DIM), 0.5),
        'state_shift': nrm((L, DEC_BATCH, B_COLS)),
        'page_table': page_table,
        'p_prompt': nrm((L, BATCH, SEQ, PLE_DIM)),
        'p_sample': nrm((L, DEC_BATCH, DEC_SEQ, PLE_DIM)),
        'mix_norm': gain((L, D_MODEL)),
        'w_in': nrm((L, D_MODEL, IN_COLS), D_MODEL ** -0.5),
        'a_ln_g': gain((L, A_WIDTH)),
        'a_ln_b': nrm((L, A_WIDTH), 0.02),
        'a_ws': nrm((L, A_GROUPS, CHUNK, CHUNK), 0.5 * CHUNK ** -0.5),
        'a_bs': 1.0 + nrm((L, A_GROUPS, CHUNK), 0.1),
        'b_mu': unif((L, B_COLS), 0.0, 1.0),
        'b_w0': unif((L, B_WIDTH), -6.0, -1.0),
        'b_w2': nrm((L, DECAY_LORA, B_WIDTH), 0.5 * DECAY_LORA ** -0.5),
        'b_a0': nrm((L, B_WIDTH), 0.5),
        'b_a2': nrm((L, ICLR_LORA, B_WIDTH), 0.5 * ICLR_LORA ** -0.5),
        'b_g2': nrm((L, GATE_LORA, B_WIDTH), GATE_LORA ** -0.5),
        'b_kk': 0.85 + nrm((L, B_WIDTH), 0.05),
        'b_ka': 1.0 + nrm((L, B_WIDTH), 0.05),
        'b_rk': nrm((L, B_HEADS, HEAD_DIM), 0.1),
        'b_ln_g': gain((L, B_WIDTH)),
        'b_ln_b': nrm((L, B_WIDTH), 0.02),
        'c_qn': gain((L, HEAD_DIM)),
        'c_kn': gain((L, HEAD_DIM)),
        'c_bias': SB_BIAS_INIT + nrm((L, C_HEADS), 0.1),
        'w_out': nrm((L, MIX_WIDTH, D_MODEL), MIX_WIDTH ** -0.5),
        'ffn_norm': gain((L, D_MODEL)),
        'w_gate': nrm((L, D_MODEL, D_FF), D_MODEL ** -0.5),
        'w_up': nrm((L, D_MODEL, D_FF), D_MODEL ** -0.5),
        'w_down': nrm((L, D_FF, D_MODEL), D_FF ** -0.5),
        'ple_norm': gain((L, D_MODEL)),
        'w_ple_gate': nrm((L, D_MODEL, D_MODEL), D_MODEL ** -0.5),
        'w_ple': nrm((L, PLE_DIM, D_MODEL), PLE_DIM ** -0.5),
    }


def reference(x_prompt, x_sample, cache_k, cache_v, state_wkv, state_shift, page_table, p_prompt, p_sample,
              mix_norm, w_in, a_ln_g, a_ln_b, a_ws, a_bs, b_mu, b_w0, b_w2, b_a0, b_a2, b_g2, b_kk, b_ka,
              b_rk, b_ln_g, b_ln_b, c_qn, c_kn, c_bias, w_out, ffn_norm, w_gate, w_up, w_down, ple_norm,
              w_ple_gate, w_ple):
    P = dict(mix_norm=mix_norm, w_in=w_in, a_ln_g=a_ln_g, a_ln_b=a_ln_b, a_ws=a_ws, a_bs=a_bs,
             b_mu=b_mu, b_w0=b_w0, b_w2=b_w2, b_a0=b_a0, b_a2=b_a2, b_g2=b_g2, b_kk=b_kk, b_ka=b_ka,
             b_rk=b_rk, b_ln_g=b_ln_g, b_ln_b=b_ln_b, c_qn=c_qn, c_kn=c_kn, c_bias=c_bias, w_out=w_out,
             ffn_norm=ffn_norm, w_gate=w_gate, w_up=w_up, w_down=w_down, ple_norm=ple_norm,
             w_ple_gate=w_ple_gate, w_ple=w_ple)
    bp = x_prompt.shape[0]
    shift0_p = jnp.zeros((DEPTH, bp, B_COLS), x_prompt.dtype)
    wkv0_p = jnp.zeros((DEPTH, bp, B_HEADS, HEAD_DIM, HEAD_DIM), jnp.float32)
    y_prompt, k_prompt, v_prompt, wkv_prompt, shift_prompt, _ = run_trunk(
        x_prompt, p_prompt, shift0_p, wkv0_p, None, P)
    y_sample, k_sample, v_sample, wkv_sample, shift_sample, av_sample = run_trunk(
        x_sample, p_sample, state_shift, state_wkv, (cache_k, cache_v, page_table), P)
    return (y_prompt, y_sample, k_prompt, v_prompt, wkv_prompt, shift_prompt,
            k_sample, v_sample, wkv_sample, shift_sample, av_sample)
```

```python
import functools

import jax
import jax.numpy as jnp
from jax import lax
from jax.experimental import pallas as pl
from jax.experimental.pallas import tpu as pltpu

F32 = jnp.float32
BF16 = jnp.bfloat16

HEAD_DIM = 64
A_WIDTH = 256
A_GROUPS = 4
CHUNK = 128
B_WIDTH = 512
B_HEADS = 8
B_COLS = 1792
LORA_WA = 128
C_WIDTH = 256
C_HEADS = 4
SB_BLOCK = 128
SB_SCALE = HEAD_DIM ** -0.5
RMS_EPS = 1e-6
LN_EPS = 1e-5
GN_EPS = 64e-5
RW_CHUNK = 64

VMEM_LIMIT = 56 * 1024 * 1024


def _dot(a, b):
    return jnp.dot(a, b, preferred_element_type=F32)


def _dot_nt(a, b):
    return lax.dot_general(a, b, (((1,), (1,)), ((), ())), preferred_element_type=F32)


def _dot_tn(a, b):
    return lax.dot_general(a, b, (((0,), (0,)), ((), ())), preferred_element_type=F32)


def _split_dot(x, m):
    hi = x.astype(BF16)
    lo = (x - hi.astype(F32)).astype(BF16)
    return _dot(hi, m) + _dot(lo, m)


def _rms(x, g):
    return x * lax.rsqrt(jnp.mean(x * x, axis=-1, keepdims=True) + RMS_EPS) * g


def _softplus(x):
    return jnp.maximum(x, 0.0) + jnp.log1p(jnp.exp(-jnp.abs(x)))


def _const_spec(shape):
    nd = len(shape)
    return pl.BlockSpec(shape, lambda *_: (0,) * nd, pipeline_mode=pl.Buffered(1))


def _proj_kernel(sample, h_ref, g_ref, w_ref, lng_ref, lnb_ref, ws_ref, bs_ref, qn_ref, kn_ref, *outs):
    if sample:
        ya_ref, pb_ref, q_ref, k_ref, v_ref, av_ref = outs
    else:
        ya_ref, pb_ref, qh_ref, kh_ref, vh_ref, k_ref, v_ref = outs
    tm = h_ref.shape[0]
    hn = _rms(h_ref[...], g_ref[...]).astype(BF16)

    pa = _dot(hn, w_ref[:, 0:2 * A_WIDTH])
    a_u = pa[:, :A_WIDTH]
    a_v = pa[:, A_WIDTH:]
    xc = a_v - jnp.mean(a_v, axis=-1, keepdims=True)
    a_vn = xc * lax.rsqrt(jnp.mean(xc * xc, axis=-1, keepdims=True) + LN_EPS) * lng_ref[...] + lnb_ref[...]
    if sample:
        av_ref[...] = a_vn
        ya_ref[...] = (a_u * (ws_ref[...] * a_vn + bs_ref[...])).astype(BF16)
    else:
        causal = lax.broadcasted_iota(jnp.int32, (CHUNK, CHUNK), 0) >= lax.broadcasted_iota(jnp.int32, (CHUNK, CHUNK), 1)
        group = lax.broadcasted_iota(jnp.int32, (CHUNK, A_WIDTH), 1) // HEAD_DIM
        w_tril = [jnp.where(causal, ws_ref[g], 0.0).astype(BF16) for g in range(A_GROUPS)]
        for c in range(tm // CHUNK):
            rows = slice(c * CHUNK, (c + 1) * CHUNK)
            v_c = a_vn[rows]
            mixed = bs_ref[...]
            for g in range(A_GROUPS):
                mixed = mixed + _dot(w_tril[g], jnp.where(group == g, v_c, 0.0).astype(BF16))
            ya_ref[rows, :] = (a_u[rows] * mixed).astype(BF16)

    pb_ref[...] = _dot(hn, w_ref[:, 2 * A_WIDTH:2 * A_WIDTH + B_COLS])

    pc = _dot(hn, w_ref[:, 2 * A_WIDTH + B_COLS:])
    v_ref[...] = pc[:, 2 * C_WIDTH:]
    for h in range(C_HEADS):
        lanes = slice(h * HEAD_DIM, (h + 1) * HEAD_DIM)
        q_h = _rms(pc[:, lanes], qn_ref[...]) * SB_SCALE
        k_h = _rms(pc[:, C_WIDTH + h * HEAD_DIM:C_WIDTH + (h + 1) * HEAD_DIM], kn_ref[...])
        k_ref[:, lanes] = k_h
        if sample:
            q_ref[:, lanes] = q_h
        else:
            qh_ref[h] = q_h.astype(BF16)
            kh_ref[h] = k_h.astype(BF16)
            vh_ref[h] = pc[:, 2 * C_WIDTH + h * HEAD_DIM:2 * C_WIDTH + (h + 1) * HEAD_DIM].astype(BF16)


def _proj(h, g, w, lng, lnb, ws, bs, qn, kn, *, sample, tm):
    m, d = h.shape
    row = lambda width: pl.BlockSpec((tm, width), lambda i: (i, 0))
    head = pl.BlockSpec((C_HEADS, tm, HEAD_DIM), lambda i: (0, i, 0))
    in_specs = [row(d), _const_spec(g.shape), _const_spec(w.shape), _const_spec(lng.shape), _const_spec(lnb.shape),
                _const_spec(ws.shape), _const_spec(bs.shape), _const_spec(qn.shape), _const_spec(kn.shape)]
    if sample:
        out_shape = [jax.ShapeDtypeStruct((m, A_WIDTH), BF16), jax.ShapeDtypeStruct((m, B_COLS), F32),
                     jax.ShapeDtypeStruct((m, C_WIDTH), F32), jax.ShapeDtypeStruct((m, C_WIDTH), F32),
                     jax.ShapeDtypeStruct((m, C_WIDTH), F32), jax.ShapeDtypeStruct((m, A_WIDTH), F32)]
        out_specs = [row(A_WIDTH), row(B_COLS), row(C_WIDTH), row(C_WIDTH), row(C_WIDTH), row(A_WIDTH)]
    else:
        hshape = jax.ShapeDtypeStruct((C_HEADS, m, HEAD_DIM), BF16)
        out_shape = [jax.ShapeDtypeStruct((m, A_WIDTH), BF16), jax.ShapeDtypeStruct((m, B_COLS), F32),
                     hshape, hshape, hshape,
                     jax.ShapeDtypeStruct((m, C_WIDTH), F32), jax.ShapeDtypeStruct((m, C_WIDTH), F32)]
        out_specs = [row(A_WIDTH), row(B_COLS), head, head, head, row(C_WIDTH), row(C_WIDTH)]
    return pl.pallas_call(
        functools.partial(_proj_kernel, sample),
        grid=(m // tm,),
        in_specs=in_specs,
        out_specs=out_specs,
        out_shape=out_shape,
        compiler_params=pltpu.CompilerParams(dimension_semantics=("arbitrary",), vmem_limit_bytes=VMEM_LIMIT),
        name="proj_sample" if sample else "proj_prompt",
    )(h, g, w, lng, lnb, ws, bs, qn, kn)


def _rwkv_prep(x, prev, mu, w0, w2p, a0, a2p, g2, k_k, k_a, bd):
    xs = x + (prev - x) * mu
    r = xs[:, 0:B_WIDTH]
    k = xs[:, B_WIDTH:2 * B_WIDTH]
    v = xs[:, 2 * B_WIDTH:3 * B_WIDTH]
    wa = xs[:, 3 * B_WIDTH:3 * B_WIDTH + LORA_WA]
    gl = xs[:, 3 * B_WIDTH + LORA_WA:]
    w_log = -_softplus(-(w0 + _dot(jnp.tanh(wa).astype(BF16), w2p))) - 0.5
    log_decay = -jnp.exp(w_log)
    a = jax.nn.sigmoid(a0 + _dot(wa.astype(BF16), a2p))
    g = _dot(jax.nn.sigmoid(gl).astype(BF16), g2)
    kk = k * k_k
    kk = kk * lax.rsqrt(_split_dot(kk * kk, bd) + 1e-12)
    k2 = k * (1.0 + (a - 1.0) * k_a)
    return r, k2, v, kk, a, log_decay, g


def _rwkv_post(y, r, k2, v, g, r_k, ln_g, ln_b, bd):
    yc = y - _split_dot(y, bd) * (1.0 / HEAD_DIM)
    yn = yc * lax.rsqrt(_split_dot(yc * yc, bd) * (1.0 / HEAD_DIM) + GN_EPS) * ln_g + ln_b
    bonus = _split_dot(r * k2 * r_k, bd) * v
    return (yn + bonus) * g


def _rwkv_prompt_kernel(pb_ref, mu_ref, w0_ref, w2_ref, a0_ref, a2_ref, g2_ref, kk_ref, ka_ref, rk_ref, lng_ref,
                        lnb_ref, bd_ref, y_ref, s_ref,
                        prev_ref, at_ref, rt_ref, bt_ref, kt_ref, vv_ref, pc_ref, o_ref, post_ref):
    tt = pb_ref.shape[0]
    n_chunks = tt // RW_CHUNK
    t_idx = pl.program_id(1)

    @pl.when(t_idx == 0)
    def _():
        prev_ref[...] = jnp.zeros_like(prev_ref)
        s_ref[...] = jnp.zeros_like(s_ref)

    x = pb_ref[...]
    row_id = lax.broadcasted_iota(jnp.int32, x.shape, 0)
    prev = jnp.where(row_id == 0, prev_ref[0:1, :], pltpu.roll(x, 1, axis=0))
    prev_ref[0:1, :] = x[tt - 1:tt, :]
    bd = bd_ref[...]
    r, k2, v, kk, a, log_decay, g = _rwkv_prep(x, prev, mu_ref[...], w0_ref[...], w2_ref[...], a0_ref[...],
                                               a2_ref[...], g2_ref[...], kk_ref[...], ka_ref[...], bd)

    ti = lax.broadcasted_iota(jnp.int32, (tt, tt), 0)
    si = lax.broadcasted_iota(jnp.int32, (tt, tt), 1)
    same_chunk_lower = ((ti // RW_CHUNK) == (si // RW_CHUNK)) & (si <= ti)
    cum = _split_dot_left(jnp.where(same_chunk_lower, 1.0, 0.0).astype(BF16), log_decay)
    p_in = jnp.exp(cum)
    p_ex = jnp.exp(cum - log_decay)
    p_inv = jnp.exp(-cum)
    a_t = -kk * p_ex
    b_t = kk * a * p_inv
    k_t = k2 * p_inv
    r_t = r * p_in
    for h in range(B_HEADS):
        lanes = slice(h * HEAD_DIM, (h + 1) * HEAD_DIM)
        at_ref[h] = a_t[:, lanes].astype(BF16)
        rt_ref[h] = r_t[:, lanes]
        bt_ref[h] = b_t[:, lanes].astype(BF16)
        kt_ref[h] = k_t[:, lanes].astype(BF16)
        vv_ref[h] = v[:, lanes].astype(BF16)
    for c in range(n_chunks):
        last = (c + 1) * RW_CHUNK - 1
        pc_ref[c:c + 1, :] = p_in[last:last + 1, :]
    post_ref[0] = _split_dot(r * k2 * rk_ref[...], bd) * v
    post_ref[1] = g

    ci = lax.broadcasted_iota(jnp.int32, (RW_CHUNK, RW_CHUNK), 0)
    cj = lax.broadcasted_iota(jnp.int32, (RW_CHUNK, RW_CHUNK), 1)
    strict = cj < ci
    incl = cj <= ci
    eye = jnp.where(ci == cj, 1.0, 0.0)

    def chunk_body(c, _):
        rows = pl.ds(pl.multiple_of(c * RW_CHUNK, RW_CHUNK), RW_CHUNK)
        pc_row = pc_ref[pl.ds(c, 1), :]
        for h in range(B_HEADS):
            at = at_ref[h, rows, :]
            rt = rt_ref[h, rows, :]
            bt = bt_ref[h, rows, :]
            kt = kt_ref[h, rows, :]
            vv = vv_ref[h, rows, :]
            xq = jnp.concatenate([at, rt.astype(BF16)], axis=0)
            ab = _dot_nt(xq, bt)
            ak = _dot_nt(xq, kt)
            a_ab = jnp.where(strict, ab[:RW_CHUNK], 0.0)
            a_rb = jnp.where(incl, ab[RW_CHUNK:], 0.0).astype(BF16)
            a_ak = jnp.where(strict, ak[:RW_CHUNK], 0.0).astype(BF16)
            a_rk = jnp.where(incl, ak[RW_CHUNK:], 0.0).astype(BF16)
            t_inv = eye + a_ab
            n_pow = a_ab
            for _ in range(5):
                n_b = n_pow.astype(BF16)
                n_pow = _dot(n_b, n_b)
                t_inv = t_inv + _dot(n_pow.astype(BF16), t_inv.astype(BF16))
            t_b = t_inv.astype(BF16)
            w_m = _dot(t_b, at).astype(BF16)
            u0 = _dot(t_b, _dot(a_ak, vv).astype(BF16)).astype(BF16)
            m_p = _dot_tn(bt, w_m).astype(BF16)
            g_t = _dot_tn(u0, bt) + _dot_tn(vv, kt)
            r_p = (rt + _dot(a_rb, w_m)).astype(BF16)
            o_loc = _dot(a_rb, u0) + _dot(a_rk, vv)
            s0 = s_ref[0, h]
            s0b = s0.astype(BF16)
            o_ref[rows, h * HEAD_DIM:(h + 1) * HEAD_DIM] = _dot_nt(r_p, s0b) + o_loc
            s_ref[0, h] = (s0 + _dot_nt(s0b, m_p) + g_t) * pc_row[:, h * HEAD_DIM:(h + 1) * HEAD_DIM]
        return 0

    lax.fori_loop(0, n_chunks, chunk_body, 0)

    y = o_ref[...]
    yc = y - _split_dot(y, bd) * (1.0 / HEAD_DIM)
    yn = yc * lax.rsqrt(_split_dot(yc * yc, bd) * (1.0 / HEAD_DIM) + GN_EPS) * lng_ref[...] + lnb_ref[...]
    y_ref[...] = ((yn + post_ref[0]) * post_ref[1]).astype(BF16)


def _split_dot_left(m, x):
    hi = x.astype(BF16)
    lo = (x - hi.astype(F32)).astype(BF16)
    return _dot(m, hi) + _dot(m, lo)


def _rwkv_prompt(pb, params, *, batch, seq, tt):
    m = pb.shape[0]
    nt = seq // tt
    in_specs = [pl.BlockSpec((tt, B_COLS), lambda b, t: (b * nt + t, 0))] + [_const_spec(p.shape) for p in params]
    head_scratch = lambda dt: pltpu.VMEM((B_HEADS, tt, HEAD_DIM), dt)
    return pl.pallas_call(
        _rwkv_prompt_kernel,
        grid=(batch, nt),
        in_specs=in_specs,
        out_specs=[pl.BlockSpec((tt, B_WIDTH), lambda b, t: (b * nt + t, 0)),
                   pl.BlockSpec((1, B_HEADS, HEAD_DIM, HEAD_DIM), lambda b, t: (b, 0, 0, 0))],
        out_shape=[jax.ShapeDtypeStruct((m, B_WIDTH), BF16),
                   jax.ShapeDtypeStruct((batch, B_HEADS, HEAD_DIM, HEAD_DIM), F32)],
        scratch_shapes=[pltpu.VMEM((8, B_COLS), F32), head_scratch(BF16), head_scratch(F32), head_scratch(BF16),
                        head_scratch(BF16), head_scratch(BF16), pltpu.VMEM((8, B_WIDTH), F32),
                        pltpu.VMEM((tt, B_WIDTH), F32), pltpu.VMEM((2, tt, B_WIDTH), F32)],
        compiler_params=pltpu.CompilerParams(dimension_semantics=("arbitrary", "arbitrary"),
                                             vmem_limit_bytes=VMEM_LIMIT),
        name="rwkv_prompt",
    )(pb, *params)


def _rwkv_step_kernel(pb_ref, sh_ref, s_ref, mu_ref, w0_ref, w2_ref, a0_ref, a2_ref, g2_ref, kk_ref, ka_ref, rk_ref,
                      lng_ref, lnb_ref, bd_ref, y_ref, so_ref, vec_ref, o_ref):
    bt = pb_ref.shape[0]
    bd = bd_ref[...]
    r, k2, v, kk, a, log_decay, g = _rwkv_prep(pb_ref[...], sh_ref[...], mu_ref[...], w0_ref[...], w2_ref[...],
                                               a0_ref[...], a2_ref[...], g2_ref[...], kk_ref[...], ka_ref[...], bd)
    vec_ref[0] = r
    vec_ref[1] = jnp.exp(log_decay)
    vec_ref[2] = k2
    vec_ref[3] = v
    vec_ref[4] = kk
    vec_ref[5] = kk * a
    eye = lax.broadcasted_iota(jnp.int32, (HEAD_DIM, HEAD_DIM), 0) == lax.broadcasted_iota(jnp.int32, (HEAD_DIM, HEAD_DIM), 1)
    for b in range(bt):
        for h in range(B_HEADS):
            lanes = slice(h * HEAD_DIM, (h + 1) * HEAD_DIM)
            r_r, w_r, k_r, v_r, kk_r, b_r = (vec_ref[i, b:b + 1, lanes] for i in range(6))
            s = s_ref[b, h]
            s_kk = jnp.sum(s * kk_r, axis=1, keepdims=True)
            v_c = jnp.sum(jnp.where(eye, v_r, 0.0), axis=1, keepdims=True)
            s_new = s * w_r - s_kk * b_r + v_c * k_r
            so_ref[b, h] = s_new
            y_c = jnp.sum(s_new * r_r, axis=1, keepdims=True)
            o_ref[b:b + 1, lanes] = jnp.sum(jnp.where(eye, y_c, 0.0), axis=0, keepdims=True)
    y_ref[...] = _rwkv_post(o_ref[...], r, k2, v, g, rk_ref[...], lng_ref[...], lnb_ref[...], bd).astype(BF16)


def _rwkv_step(pb, shift, state, params, *, bt):
    m = pb.shape[0]
    st_spec = pl.BlockSpec((bt, B_HEADS, HEAD_DIM, HEAD_DIM), lambda i: (i, 0, 0, 0))
    in_specs = [pl.BlockSpec((bt, B_COLS), lambda i: (i, 0)), pl.BlockSpec((bt, B_COLS), lambda i: (i, 0)), st_spec]
    in_specs += [_const_spec(p.shape) for p in params]
    return pl.pallas_call(
        _rwkv_step_kernel,
        grid=(m // bt,),
        in_specs=in_specs,
        out_specs=[pl.BlockSpec((bt, B_WIDTH), lambda i: (i, 0)), st_spec],
        out_shape=[jax.ShapeDtypeStruct((m, B_WIDTH), BF16), jax.ShapeDtypeStruct(state.shape, F32)],
        scratch_shapes=[pltpu.VMEM((6, bt, B_WIDTH), F32), pltpu.VMEM((bt, B_WIDTH), F32)],
        compiler_params=pltpu.CompilerParams(dimension_semantics=("arbitrary",), vmem_limit_bytes=VMEM_LIMIT),
        name="rwkv_sample",
    )(pb, shift, state, *params)


def _sb_block(z, v_b, carry, tri, mask):
    soft = jnp.log1p(jnp.exp(-jnp.abs(z)))
    log_beta = jnp.minimum(z, 0.0) - soft
    log_keep = log_beta - z
    if mask is not None:
        log_keep = jnp.where(mask, log_keep, 0.0)
    after = _split_dot(log_keep, tri) - log_keep
    att = jnp.exp(log_beta + after + carry)
    if mask is not None:
        att = jnp.where(mask, att, 0.0)
    return _dot(att.astype(BF16), v_b), carry + jnp.sum(log_keep, axis=-1, keepdims=True)


def _tri_incl(n):
    return jnp.where(lax.broadcasted_iota(jnp.int32, (n, n), 0) >= lax.broadcasted_iota(jnp.int32, (n, n), 1),
                     1.0, 0.0).astype(BF16)


def _sb_prompt_kernel(bias_ref, q_ref, k_ref, v_ref, o_ref):
    blk = q_ref.shape[1]
    i = pl.program_id(1)
    tri = _tri_incl(blk)
    causal = lax.broadcasted_iota(jnp.int32, (blk, blk), 1) < lax.broadcasted_iota(jnp.int32, (blk, blk), 0)

    def block(j, state, mask):
        off = pl.multiple_of(j * blk, blk)
        new = []
        for h in range(C_HEADS):
            acc, carry = state[h]
            z = _dot_nt(q_ref[h], k_ref[h, pl.ds(off, blk), :]) + bias_ref[h]
            pv, carry = _sb_block(z, v_ref[h, pl.ds(off, blk), :], carry, tri, mask)
            new.append((acc + pv, carry))
        return tuple(new)

    init = tuple((jnp.zeros((blk, HEAD_DIM), F32), jnp.zeros((blk, 1), F32)) for _ in range(C_HEADS))
    state = block(i, init, causal)
    state = lax.fori_loop(0, i, lambda jj, st: block(i - 1 - jj, st, None), state)
    for h in range(C_HEADS):
        o_ref[:, h * HEAD_DIM:(h + 1) * HEAD_DIM] = state[h][0].astype(BF16)


def _sb_prompt(bias, qh, kh, vh, *, batch, seq):
    m = qh.shape[1]
    nb = seq // SB_BLOCK
    kv_spec = pl.BlockSpec((C_HEADS, seq, HEAD_DIM), lambda b, i: (0, b, 0))
    return pl.pallas_call(
        _sb_prompt_kernel,
        grid=(batch, nb),
        in_specs=[pl.BlockSpec(memory_space=pltpu.SMEM),
                  pl.BlockSpec((C_HEADS, SB_BLOCK, HEAD_DIM), lambda b, i: (0, b * nb + i, 0)), kv_spec, kv_spec],
        out_specs=pl.BlockSpec((SB_BLOCK, C_WIDTH), lambda b, i: (b * nb + i, 0)),
        out_shape=jax.ShapeDtypeStruct((m, C_WIDTH), BF16),
        compiler_params=pltpu.CompilerParams(dimension_semantics=("arbitrary", "arbitrary"),
                                             vmem_limit_bytes=VMEM_LIMIT),
        name="sb_prompt",
    )(bias, qh, kh, vh)


def _sb_sample_kernel(n_pages, page_size, pt_ref, bias_ref, q_ref, kn_ref, vn_ref, *refs):
    k_refs = refs[:n_pages]
    v_refs = refs[n_pages:2 * n_pages]
    o_ref = refs[2 * n_pages]
    rows = 8
    r_id = lax.broadcasted_iota(jnp.int32, (rows, C_WIDTH), 0)
    head_lanes = (lax.broadcasted_iota(jnp.int32, (rows, C_WIDTH), 1) // HEAD_DIM) == r_id
    q = jnp.where(head_lanes, q_ref[0], 0.0)
    q_b = q.astype(BF16)
    r_col = lax.broadcasted_iota(jnp.int32, (rows, 1), 0)
    bias = jnp.zeros((rows, 1), F32)
    for h in range(C_HEADS):
        bias = jnp.where(r_col == h, bias_ref[h], bias)
    tri = _tri_incl(page_size)

    past = n_pages * page_size
    new_is_causal = (past + 0) < (past + 0)
    z_new = jnp.sum(q * kn_ref[0], axis=-1, keepdims=True) + bias
    keep_new = jnp.where(new_is_causal, -_softplus(z_new), 0.0)
    carry = keep_new
    acc = jnp.where(new_is_causal, jnp.exp(-_softplus(-z_new)), 0.0) * vn_ref[0]

    for p in reversed(range(n_pages)):
        z = _dot_nt(q_b, k_refs[p][0, 0].astype(BF16)) + bias
        pv, carry = _sb_block(z, v_refs[p][0, 0].astype(BF16), carry, tri, None)
        acc = acc + pv
    o_ref[0] = jnp.sum(jnp.where(head_lanes, acc, 0.0), axis=0, keepdims=True).astype(BF16)


def _sb_sample(page_table, bias, q, k_new, v_new, cache_k, cache_v, layer):
    m = q.shape[0]
    n_pages = page_table.shape[1]
    n_phys, page_size = cache_k.shape[1], cache_k.shape[2]
    ck = cache_k.reshape(cache_k.shape[0], n_phys, page_size, C_WIDTH)
    cv = cache_v.reshape(cache_v.shape[0], n_phys, page_size, C_WIDTH)
    pt = page_table.reshape(-1)
    row3 = lambda x: x.reshape(m, 1, C_WIDTH)
    row_spec = pl.BlockSpec((1, 1, C_WIDTH), lambda b, pt: (b, 0, 0))

    def page_spec(p):
        return pl.BlockSpec((1, 1, page_size, C_WIDTH), lambda b, pt: (layer, pt[b * n_pages + p], 0, 0))

    grid_spec = pltpu.PrefetchScalarGridSpec(
        num_scalar_prefetch=1,
        grid=(m,),
        in_specs=[pl.BlockSpec(memory_space=pltpu.SMEM), row_spec, row_spec, row_spec]
        + [page_spec(p) for p in range(n_pages)] + [page_spec(p) for p in range(n_pages)],
        out_specs=row_spec,
    )
    out = pl.pallas_call(
        functools.partial(_sb_sample_kernel, n_pages, page_size),
        grid_spec=grid_spec,
        out_shape=jax.ShapeDtypeStruct((m, 1, C_WIDTH), BF16),
        compiler_params=pltpu.CompilerParams(dimension_semantics=("arbitrary",), vmem_limit_bytes=VMEM_LIMIT),
        name="sb_sample",
    )(pt, bias, row3(q), row3(k_new), row3(v_new), *([ck] * n_pages), *([cv] * n_pages))
    return out.reshape(m, C_WIDTH)


def _tail_kernel(h_ref, ya_ref, yb_ref, yc_ref, pe_ref, wo_ref, fg_ref, wg_ref, wu_ref, wd_ref, pg_ref, wpg_ref,
                 wple_ref, out_ref):
    h = (h_ref[...] + _dot(ya_ref[...], wo_ref[0:A_WIDTH, :]) + _dot(yb_ref[...], wo_ref[A_WIDTH:A_WIDTH + B_WIDTH, :])
         + _dot(yc_ref[...], wo_ref[A_WIDTH + B_WIDTH:, :]))
    hn = _rms(h, fg_ref[...]).astype(BF16)
    gate = _dot(hn, wg_ref[...])
    act = (gate * jax.nn.sigmoid(gate) * _dot(hn, wu_ref[...])).astype(BF16)
    h = h + _dot(act, wd_ref[...])
    gate = jax.nn.sigmoid(_dot(_rms(h, pg_ref[...]).astype(BF16), wpg_ref[...]))
    out_ref[...] = h + _dot(pe_ref[...].astype(BF16), wple_ref[...]) * gate


def _tail(h, ya, yb, yc, pe, wo, fg, wg, wu, wd, pg, wpg, wple, *, tm):
    m, d = h.shape
    row = lambda width: pl.BlockSpec((tm, width), lambda i: (i, 0))
    weights = (wo, fg, wg, wu, wd, pg, wpg, wple)
    return pl.pallas_call(
        _tail_kernel,
        grid=(m // tm,),
        in_specs=[row(d), row(A_WIDTH), row(B_WIDTH), row(C_WIDTH), row(pe.shape[1])]
        + [_const_spec(w.shape) for w in weights],
        out_specs=row(d),
        out_shape=jax.ShapeDtypeStruct((m, d), F32),
        compiler_params=pltpu.CompilerParams(dimension_semantics=("arbitrary",), vmem_limit_bytes=VMEM_LIMIT),
        name="tail",
    )(h, ya, yb, yc, pe, *weights)


def _pad_rows(w, top, total):
    return jnp.pad(w, ((top, total - top - w.shape[0]), (0, 0)))


def kernel(x_prompt, x_sample, cache_k, cache_v, state_wkv, state_shift, page_table, p_prompt, p_sample, mix_norm, w_in, a_ln_g, a_ln_b, a_ws, a_bs, b_mu, b_w0, b_w2, b_a0, b_a2, b_g2, b_kk, b_ka, b_rk, b_ln_g, b_ln_b, c_qn, c_kn, c_bias, w_out, ffn_norm, w_gate, w_up, w_down, ple_norm, w_ple_gate, w_ple):
    depth = w_in.shape[0]
    bp, seq, d = x_prompt.shape
    bs = x_sample.shape[0]
    hp = x_prompt.reshape(bp * seq, d)
    hs = x_sample.reshape(bs, d)
    hd = lax.broadcasted_iota(jnp.int32, (B_WIDTH, B_WIDTH), 0) // HEAD_DIM
    bd = (hd == hd.T).astype(BF16)
    row = lambda x: x.reshape(1, -1)

    outs = {n: [] for n in ("kp", "vp", "wkvp", "shp", "ks", "vs", "wkvs", "shs", "avs")}
    for i in range(depth):
        w_in_b = w_in[i].astype(BF16)
        proj_common = (row(mix_norm[i]), w_in_b, row(a_ln_g[i]), row(a_ln_b[i]))
        bias_lanes = jnp.repeat(a_bs[i].T, HEAD_DIM, axis=1)
        w00_lanes = jnp.repeat(a_ws[i][:, 0, 0], HEAD_DIM).reshape(1, A_WIDTH)
        rw_params = (row(b_mu[i]), row(b_w0[i]), _pad_rows(b_w2[i], 0, LORA_WA).astype(BF16), row(b_a0[i]),
                     _pad_rows(b_a2[i], LORA_WA - b_a2.shape[1], LORA_WA).astype(BF16), b_g2[i].astype(BF16),
                     row(b_kk[i]), row(b_ka[i]), row(b_rk[i]), row(b_ln_g[i]), row(b_ln_b[i]), bd)
        tail_w = (w_out[i].astype(BF16), row(ffn_norm[i]), w_gate[i].astype(BF16), w_up[i].astype(BF16),
                  w_down[i].astype(BF16), row(ple_norm[i]), w_ple_gate[i].astype(BF16), w_ple[i].astype(BF16))

        ya, pb, qh, kh, vh, k_n, v_n = _proj(hp, *proj_common, a_ws[i], bias_lanes, row(c_qn[i]), row(c_kn[i]),
                                             sample=False, tm=512)
        yb, s_fin = _rwkv_prompt(pb, rw_params, batch=bp, seq=seq, tt=256)
        yc = _sb_prompt(c_bias[i], qh, kh, vh, batch=bp, seq=seq)
        hp = _tail(hp, ya, yb, yc, p_prompt[i].reshape(bp * seq, -1), *tail_w, tm=512)
        outs["kp"].append(k_n.reshape(bp, seq, C_HEADS, HEAD_DIM))
        outs["vp"].append(v_n.reshape(bp, seq, C_HEADS, HEAD_DIM))
        outs["wkvp"].append(s_fin)
        outs["shp"].append(pb.reshape(bp, seq, B_COLS)[:, -1])

        ya, pb, q_n, k_n, v_n, a_vn = _proj(hs, *proj_common, w00_lanes, bias_lanes[0:1], row(c_qn[i]), row(c_kn[i]),
                                            sample=True, tm=bs)
        yb, s_new = _rwkv_step(pb, state_shift[i], state_wkv[i], rw_params, bt=8)
        yc = _sb_sample(page_table, c_bias[i], q_n, k_n, v_n, cache_k, cache_v, i)
        hs = _tail(hs, ya, yb, yc, p_sample[i].reshape(bs, -1), *tail_w, tm=bs)
        outs["ks"].append(k_n.reshape(bs, 1, C_HEADS, HEAD_DIM))
        outs["vs"].append(v_n.reshape(bs, 1, C_HEADS, HEAD_DIM))
        outs["wkvs"].append(s_new)
        outs["shs"].append(pb)
        outs["avs"].append(a_vn.reshape(bs, 1, A_WIDTH))

    st = {n: jnp.stack(v) for n, v in outs.items()}
    return (hp.reshape(bp, seq, d), hs.reshape(bs, 1, d), st["kp"], st["vp"], st["wkvp"], st["shp"],
            st["ks"], st["vs"], st["wkvs"], st["shs"], st["avs"])
```

```python
import functools

import jax
import jax.numpy as jnp
from jax import lax
from jax.experimental import pallas as pl
from jax.experimental.pallas import tpu as pltpu

F32 = jnp.float32
BF16 = jnp.bfloat16

HEAD_DIM = 64
A_WIDTH = 256
A_GROUPS = 4
CHUNK = 128
B_WIDTH = 512
B_HEADS = 8
B_COLS = 1792
LORA_WA = 128
C_WIDTH = 256
C_HEADS = 4
SB_BLOCK = 128
SB_SCALE = HEAD_DIM ** -0.5
RMS_EPS = 1e-6
LN_EPS = 1e-5
GN_EPS = 64e-5
RW_CHUNK = 64

VMEM_LIMIT = 56 * 1024 * 1024


def _dot(a, b):
    return jnp.dot(a, b, preferred_element_type=F32)


def _dot_nt(a, b):
    return lax.dot_general(a, b, (((1,), (1,)), ((), ())), preferred_element_type=F32)


def _dot_tn(a, b):
    return lax.dot_general(a, b, (((0,), (0,)), ((), ())), preferred_element_type=F32)


def _split_dot(x, m):
    hi = x.astype(BF16)
    lo = (x - hi.astype(F32)).astype(BF16)
    return _dot(hi, m) + _dot(lo, m)


def _rms(x, g):
    return x * lax.rsqrt(jnp.mean(x * x, axis=-1, keepdims=True) + RMS_EPS) * g


def _softplus(x):
    return jnp.maximum(x, 0.0) + jnp.log1p(jnp.exp(-jnp.abs(x)))


def _const_spec(shape):
    nd = len(shape)
    return pl.BlockSpec(shape, lambda *_: (0,) * nd, pipeline_mode=pl.Buffered(1))


def _proj_kernel(sample, h_ref, g_ref, w_ref, lng_ref, lnb_ref, ws_ref, bs_ref, qn_ref, kn_ref, *outs):
    if sample:
        ya_ref, pb_ref, q_ref, k_ref, v_ref, av_ref = outs
    else:
        ya_ref, pb_ref, qh_ref, kh_ref, vh_ref, k_ref, v_ref = outs
    tm = h_ref.shape[0]
    hn = _rms(h_ref[...], g_ref[...]).astype(BF16)

    pa = _dot(hn, w_ref[:, 0:2 * A_WIDTH])
    a_u = pa[:, :A_WIDTH]
    a_v = pa[:, A_WIDTH:]
    xc = a_v - jnp.mean(a_v, axis=-1, keepdims=True)
    a_vn = xc * lax.rsqrt(jnp.mean(xc * xc, axis=-1, keepdims=True) + LN_EPS) * lng_ref[...] + lnb_ref[...]
    if sample:
        av_ref[...] = a_vn
        ya_ref[...] = (a_u * (ws_ref[...] * a_vn + bs_ref[...])).astype(BF16)
    else:
        causal = lax.broadcasted_iota(jnp.int32, (CHUNK, CHUNK), 0) >= lax.broadcasted_iota(jnp.int32, (CHUNK, CHUNK), 1)
        group = lax.broadcasted_iota(jnp.int32, (CHUNK, A_WIDTH), 1) // HEAD_DIM
        w_tril = [jnp.where(causal, ws_ref[g], 0.0).astype(BF16) for g in range(A_GROUPS)]
        for c in range(tm // CHUNK):
            rows = slice(c * CHUNK, (c + 1) * CHUNK)
            v_c = a_vn[rows]
            mixed = bs_ref[...]
            for g in range(A_GROUPS):
                mixed = mixed + _dot(w_tril[g], jnp.where(group == g, v_c, 0.0).astype(BF16))
            ya_ref[rows, :] = (a_u[rows] * mixed).astype(BF16)

    pb_ref[...] = _dot(hn, w_ref[:, 2 * A_WIDTH:2 * A_WIDTH + B_COLS])

    pc = _dot(hn, w_ref[:, 2 * A_WIDTH + B_COLS:])
    v_ref[...] = pc[:, 2 * C_WIDTH:]
    for h in range(C_HEADS):
        lanes = slice(h * HEAD_DIM, (h + 1) * HEAD_DIM)
        q_h = _rms(pc[:, lanes], qn_ref[...]) * SB_SCALE
        k_h = _rms(pc[:, C_WIDTH + h * HEAD_DIM:C_WIDTH + (h + 1) * HEAD_DIM], kn_ref[...])
        k_ref[:, lanes] = k_h
        if sample:
            q_ref[:, lanes] = q_h
        else:
            qh_ref[h] = q_h.astype(BF16)
            kh_ref[h] = k_h.astype(BF16)
            vh_ref[h] = pc[:, 2 * C_WIDTH + h * HEAD_DIM:2 * C_WIDTH + (h + 1) * HEAD_DIM].astype(BF16)


def _proj(h, g, w, lng, lnb, ws, bs, qn, kn, *, sample, tm):
    m, d = h.shape
    row = lambda width: pl.BlockSpec((tm, width), lambda i: (i, 0))
    head = pl.BlockSpec((C_HEADS, tm, HEAD_DIM), lambda i: (0, i, 0))
    in_specs = [row(d), _const_spec(g.shape), _const_spec(w.shape), _const_spec(lng.shape), _const_spec(lnb.shape),
                _const_spec(ws.shape), _const_spec(bs.shape), _const_spec(qn.shape), _const_spec(kn.shape)]
    if sample:
        out_shape = [jax.ShapeDtypeStruct((m, A_WIDTH), BF16), jax.ShapeDtypeStruct((m, B_COLS), F32),
                     jax.ShapeDtypeStruct((m, C_WIDTH), F32), jax.ShapeDtypeStruct((m, C_WIDTH), F32),
                     jax.ShapeDtypeStruct((m, C_WIDTH), F32), jax.ShapeDtypeStruct((m, A_WIDTH), F32)]
        out_specs = [row(A_WIDTH), row(B_COLS), row(C_WIDTH), row(C_WIDTH), row(C_WIDTH), row(A_WIDTH)]
    else:
        hshape = jax.ShapeDtypeStruct((C_HEADS, m, HEAD_DIM), BF16)
        out_shape = [jax.ShapeDtypeStruct((m, A_WIDTH), BF16), jax.ShapeDtypeStruct((m, B_COLS), F32),
                     hshape, hshape, hshape,
                     jax.ShapeDtypeStruct((m, C_WIDTH), F32), jax.ShapeDtypeStruct((m, C_WIDTH), F32)]
        out_specs = [row(A_WIDTH), row(B_COLS), head, head, head, row(C_WIDTH), row(C_WIDTH)]
    return pl.pallas_call(
        functools.partial(_proj_kernel, sample),
        grid=(m // tm,),
        in_specs=in_specs,
        out_specs=out_specs,
        out_shape=out_shape,
        compiler_params=pltpu.CompilerParams(dimension_semantics=("arbitrary",), vmem_limit_bytes=VMEM_LIMIT),
        name="proj_sample" if sample else "proj_prompt",
    )(h, g, w, lng, lnb, ws, bs, qn, kn)


def _rwkv_prep(x, prev, mu, w0, w2p, a0, a2p, g2, k_k, k_a, bd):
    xs = x + (prev - x) * mu
    r = xs[:, 0:B_WIDTH]
    k = xs[:, B_WIDTH:2 * B_WIDTH]
    v = xs[:, 2 * B_WIDTH:3 * B_WIDTH]
    wa = xs[:, 3 * B_WIDTH:3 * B_WIDTH + LORA_WA]
    gl = xs[:, 3 * B_WIDTH + LORA_WA:]
    w_log = -_softplus(-(w0 + _dot(jnp.tanh(wa).astype(BF16), w2p))) - 0.5
    log_decay = -jnp.exp(w_log)
    a = jax.nn.sigmoid(a0 + _dot(wa.astype(BF16), a2p))
    g = _dot(jax.nn.sigmoid(gl).astype(BF16), g2)
    kk = k * k_k
    kk = kk * lax.rsqrt(_split_dot(kk * kk, bd) + 1e-12)
    k2 = k * (1.0 + (a - 1.0) * k_a)
    return r, k2, v, kk, a, log_decay, g


def _rwkv_post(y, r, k2, v, g, r_k, ln_g, ln_b, bd):
    yc = y - _split_dot(y, bd) * (1.0 / HEAD_DIM)
    yn = yc * lax.rsqrt(_split_dot(yc * yc, bd) * (1.0 / HEAD_DIM) + GN_EPS) * ln_g + ln_b
    bonus = _split_dot(r * k2 * r_k, bd) * v
    return (yn + bonus) * g


def _bmm(a, b):
    return lax.dot_general(a, b, (((2,), (1,)), ((0,), (0,))), preferred_element_type=F32)


def _bmm_nt(a, b):
    return lax.dot_general(a, b, (((2,), (2,)), ((0,), (0,))), preferred_element_type=F32)


def _bmm_tn(a, b):
    return lax.dot_general(a, b, (((1,), (1,)), ((0,), (0,))), preferred_element_type=F32)


def _rwkv_prompt_kernel(pb_ref, mu_ref, w0_ref, w2_ref, a0_ref, a2_ref, g2_ref, kk_ref, ka_ref, rk_ref, lng_ref,
                        lnb_ref, bd_ref, y_ref, s_ref,
                        prev_ref, at_ref, rt_ref, bt_ref, kt_ref, vv_ref, pc_ref, o_ref, post_ref):
    tt = pb_ref.shape[0]
    n_chunks = tt // RW_CHUNK
    t_idx = pl.program_id(1)

    @pl.when(t_idx == 0)
    def _():
        prev_ref[...] = jnp.zeros_like(prev_ref)
        s_ref[...] = jnp.zeros_like(s_ref)

    x = pb_ref[...]
    row_id = lax.broadcasted_iota(jnp.int32, x.shape, 0)
    prev = jnp.where(row_id == 0, prev_ref[0:1, :], pltpu.roll(x, 1, axis=0))
    prev_ref[0:1, :] = x[tt - 1:tt, :]
    bd = bd_ref[...]
    r, k2, v, kk, a, log_decay, g = _rwkv_prep(x, prev, mu_ref[...], w0_ref[...], w2_ref[...], a0_ref[...],
                                               a2_ref[...], g2_ref[...], kk_ref[...], ka_ref[...], bd)

    ti = lax.broadcasted_iota(jnp.int32, (tt, tt), 0)
    si = lax.broadcasted_iota(jnp.int32, (tt, tt), 1)
    same_chunk_lower = ((ti // RW_CHUNK) == (si // RW_CHUNK)) & (si <= ti)
    cum = _split_dot_left(jnp.where(same_chunk_lower, 1.0, 0.0).astype(BF16), log_decay)
    p_in = jnp.exp(cum)
    p_ex = jnp.exp(cum - log_decay)
    p_inv = jnp.exp(-cum)
    a_t = -kk * p_ex
    b_t = kk * a * p_inv
    k_t = k2 * p_inv
    r_t = r * p_in
    for h in range(B_HEADS):
        lanes = slice(h * HEAD_DIM, (h + 1) * HEAD_DIM)
        cols = [z[:, lanes] for z in (a_t, r_t, b_t, k_t, v, p_in)]
        for c in range(n_chunks):
            rows = slice(c * RW_CHUNK, (c + 1) * RW_CHUNK)
            n = c * B_HEADS + h
            at_ref[n] = cols[0][rows].astype(BF16)
            rt_ref[n] = cols[1][rows]
            bt_ref[n] = cols[2][rows].astype(BF16)
            kt_ref[n] = cols[3][rows].astype(BF16)
            vv_ref[n] = cols[4][rows].astype(BF16)
            pc_ref[n] = cols[5][(c + 1) * RW_CHUNK - 1:(c + 1) * RW_CHUNK]
    post_ref[0] = _split_dot(r * k2 * rk_ref[...], bd) * v
    post_ref[1] = g

    ci = lax.broadcasted_iota(jnp.int32, (RW_CHUNK, RW_CHUNK), 0)
    cj = lax.broadcasted_iota(jnp.int32, (RW_CHUNK, RW_CHUNK), 1)
    strict = (cj < ci)[None]
    incl = (cj <= ci)[None]
    eye = jnp.where(ci == cj, 1.0, 0.0)[None]

    at = at_ref[...]
    rt = rt_ref[...]
    bt = bt_ref[...]
    kt = kt_ref[...]
    vv = vv_ref[...]
    rtb = rt.astype(BF16)
    a_ab = jnp.where(strict, _bmm_nt(at, bt), 0.0)
    a_ak = jnp.where(strict, _bmm_nt(at, kt), 0.0).astype(BF16)
    a_rb = jnp.where(incl, _bmm_nt(rtb, bt), 0.0).astype(BF16)
    a_rk = jnp.where(incl, _bmm_nt(rtb, kt), 0.0).astype(BF16)
    t_inv = eye + a_ab
    n_pow = a_ab
    for _ in range(5):
        n_b = n_pow.astype(BF16)
        n_pow = _bmm(n_b, n_b)
        t_inv = t_inv + _bmm(n_pow.astype(BF16), t_inv.astype(BF16))
    t_b = t_inv.astype(BF16)
    w_m = _bmm(t_b, at).astype(BF16)
    u0 = _bmm(t_b, _bmm(a_ak, vv).astype(BF16)).astype(BF16)
    m_p = _bmm_tn(bt, w_m).astype(BF16)
    g_t = _bmm_tn(u0, bt) + _bmm_tn(vv, kt)
    r_p = (rt + _bmm(a_rb, w_m)).astype(BF16)
    o_loc = _bmm(a_rb, u0) + _bmm(a_rk, vv)

    s = s_ref[0]
    for c in range(n_chunks):
        tiles = slice(c * B_HEADS, (c + 1) * B_HEADS)
        sb = s.astype(BF16)
        o_c = _bmm_nt(r_p[tiles], sb) + o_loc[tiles]
        s = (s + _bmm_nt(sb, m_p[tiles]) + g_t[tiles]) * pc_ref[tiles]
        for h in range(B_HEADS):
            o_ref[c * RW_CHUNK:(c + 1) * RW_CHUNK, h * HEAD_DIM:(h + 1) * HEAD_DIM] = o_c[h]
    s_ref[0] = s

    y = o_ref[...]
    yc = y - _split_dot(y, bd) * (1.0 / HEAD_DIM)
    yn = yc * lax.rsqrt(_split_dot(yc * yc, bd) * (1.0 / HEAD_DIM) + GN_EPS) * lng_ref[...] + lnb_ref[...]
    y_ref[...] = ((yn + post_ref[0]) * post_ref[1]).astype(BF16)


def _split_dot_left(m, x):
    hi = x.astype(BF16)
    lo = (x - hi.astype(F32)).astype(BF16)
    return _dot(m, hi) + _dot(m, lo)


def _rwkv_prompt(pb, params, *, batch, seq, tt):
    m = pb.shape[0]
    nt = seq // tt
    in_specs = [pl.BlockSpec((tt, B_COLS), lambda b, t: (b * nt + t, 0))] + [_const_spec(p.shape) for p in params]
    n_tiles = B_HEADS * tt // RW_CHUNK
    head_scratch = lambda dt: pltpu.VMEM((n_tiles, RW_CHUNK, HEAD_DIM), dt)
    return pl.pallas_call(
        _rwkv_prompt_kernel,
        grid=(batch, nt),
        in_specs=in_specs,
        out_specs=[pl.BlockSpec((tt, B_WIDTH), lambda b, t: (b * nt + t, 0)),
                   pl.BlockSpec((1, B_HEADS, HEAD_DIM, HEAD_DIM), lambda b, t: (b, 0, 0, 0))],
        out_shape=[jax.ShapeDtypeStruct((m, B_WIDTH), BF16),
                   jax.ShapeDtypeStruct((batch, B_HEADS, HEAD_DIM, HEAD_DIM), F32)],
        scratch_shapes=[pltpu.VMEM((8, B_COLS), F32), head_scratch(BF16), head_scratch(F32), head_scratch(BF16),
                        head_scratch(BF16), head_scratch(BF16), pltpu.VMEM((n_tiles, 1, HEAD_DIM), F32),
                        pltpu.VMEM((tt, B_WIDTH), F32), pltpu.VMEM((2, tt, B_WIDTH), F32)],
        compiler_params=pltpu.CompilerParams(dimension_semantics=("arbitrary", "arbitrary"),
                                             vmem_limit_bytes=VMEM_LIMIT),
        name="rwkv_prompt",
    )(pb, *params)


def _rwkv_step_kernel(pb_ref, sh_ref, s_ref, mu_ref, w0_ref, w2_ref, a0_ref, a2_ref, g2_ref, kk_ref, ka_ref, rk_ref,
                      lng_ref, lnb_ref, bd_ref, y_ref, so_ref, vec_ref, o_ref):
    bt = pb_ref.shape[0]
    bd = bd_ref[...]
    r, k2, v, kk, a, log_decay, g = _rwkv_prep(pb_ref[...], sh_ref[...], mu_ref[...], w0_ref[...], w2_ref[...],
                                               a0_ref[...], a2_ref[...], g2_ref[...], kk_ref[...], ka_ref[...], bd)
    vec_ref[0] = r
    vec_ref[1] = jnp.exp(log_decay)
    vec_ref[2] = k2
    vec_ref[3] = v
    vec_ref[4] = kk
    vec_ref[5] = kk * a
    eye = lax.broadcasted_iota(jnp.int32, (HEAD_DIM, HEAD_DIM), 0) == lax.broadcasted_iota(jnp.int32, (HEAD_DIM, HEAD_DIM), 1)
    for b in range(bt):
        for h in range(B_HEADS):
            lanes = slice(h * HEAD_DIM, (h + 1) * HEAD_DIM)
            r_r, w_r, k_r, v_r, kk_r, b_r = (vec_ref[i, b:b + 1, lanes] for i in range(6))
            s = s_ref[b, h]
            s_kk = jnp.sum(s * kk_r, axis=1, keepdims=True)
            v_c = jnp.sum(jnp.where(eye, v_r, 0.0), axis=1, keepdims=True)
            s_new = s * w_r - s_kk * b_r + v_c * k_r
            so_ref[b, h] = s_new
            y_c = jnp.sum(s_new * r_r, axis=1, keepdims=True)
            o_ref[b:b + 1, lanes] = jnp.sum(jnp.where(eye, y_c, 0.0), axis=0, keepdims=True)
    y_ref[...] = _rwkv_post(o_ref[...], r, k2, v, g, rk_ref[...], lng_ref[...], lnb_ref[...], bd).astype(BF16)


def _rwkv_step(pb, shift, state, params, *, bt):
    m = pb.shape[0]
    st_spec = pl.BlockSpec((bt, B_HEADS, HEAD_DIM, HEAD_DIM), lambda i: (i, 0, 0, 0))
    in_specs = [pl.BlockSpec((bt, B_COLS), lambda i: (i, 0)), pl.BlockSpec((bt, B_COLS), lambda i: (i, 0)), st_spec]
    in_specs += [_const_spec(p.shape) for p in params]
    return pl.pallas_call(
        _rwkv_step_kernel,
        grid=(m // bt,),
        in_specs=in_specs,
        out_specs=[pl.BlockSpec((bt, B_WIDTH), lambda i: (i, 0)), st_spec],
        out_shape=[jax.ShapeDtypeStruct((m, B_WIDTH), BF16), jax.ShapeDtypeStruct(state.shape, F32)],
        scratch_shapes=[pltpu.VMEM((6, bt, B_WIDTH), F32), pltpu.VMEM((bt, B_WIDTH), F32)],
        compiler_params=pltpu.CompilerParams(dimension_semantics=("arbitrary",), vmem_limit_bytes=VMEM_LIMIT),
        name="rwkv_sample",
    )(pb, shift, state, *params)


def _sb_chains(z_chains, mask_chains, carries, tri):
    units = [(c, n) for c in range(len(z_chains)) for n in range(len(z_chains[c]))]
    log_beta, log_keep, after, att = {}, {}, {}, {}
    for c, n in units:
        z = z_chains[c][n]
        log_beta[c, n] = jnp.minimum(z, 0.0) - jnp.log1p(jnp.exp(-jnp.abs(z)))
        keep = log_beta[c, n] - z
        log_keep[c, n] = keep if mask_chains[c][n] is None else jnp.where(mask_chains[c][n], keep, 0.0)
    for u in units:
        after[u] = _dot(log_keep[u].astype(BF16), tri)
    carries = list(carries)
    for c, n in units:
        w = jnp.exp(log_beta[c, n] + after[c, n] + carries[c])
        att[c, n] = (w if mask_chains[c][n] is None else jnp.where(mask_chains[c][n], w, 0.0)).astype(BF16)
        carries[c] = carries[c] + jnp.sum(log_keep[c, n], axis=-1, keepdims=True)
    return [[att[c, n] for n in range(len(z_chains[c]))] for c in range(len(z_chains))], carries


def _tri_strict(n):
    return jnp.where(lax.broadcasted_iota(jnp.int32, (n, n), 0) > lax.broadcasted_iota(jnp.int32, (n, n), 1),
                     1.0, 0.0).astype(BF16)


def _sb_prompt_kernel(bias_ref, q_ref, k_ref, v_ref, o_ref, acc_ref, carry_ref):
    blk = q_ref.shape[1]
    i = pl.program_id(1)
    tri = _tri_strict(blk)
    causal = lax.broadcasted_iota(jnp.int32, (blk, blk), 1) < lax.broadcasted_iota(jnp.int32, (blk, blk), 0)
    acc_ref[...] = jnp.zeros_like(acc_ref)
    carry_ref[...] = jnp.zeros_like(carry_ref)

    def process(blocks, masks):
        offs = [pl.multiple_of(j * blk, blk) for j in blocks]
        zs = [[_dot_nt(q_ref[h], k_ref[h, pl.ds(off, blk), :]) + bias_ref[h] for off in offs] for h in range(C_HEADS)]
        att, carry = _sb_chains(zs, [masks] * C_HEADS, [carry_ref[h] for h in range(C_HEADS)], tri)
        pv = [[_dot(att[h][n], v_ref[h, pl.ds(off, blk), :]) for n, off in enumerate(offs)] for h in range(C_HEADS)]
        for h in range(C_HEADS):
            acc_ref[h] += functools.reduce(lambda a, b: a + b, pv[h])
            carry_ref[h] = carry[h]

    @pl.when(i == 0)
    def _():
        process([i], [causal])

    @pl.when(i > 0)
    def _():
        process([i, i - 1], [causal, None])

        def pair(p, _):
            process([i - 2 - 2 * p, i - 3 - 2 * p], [None, None])
            return 0

        lax.fori_loop(0, (i - 1) // 2, pair, 0)

    @pl.when((i > 0) & ((i - 1) % 2 == 1))
    def _():
        process([0], [None])

    for h in range(C_HEADS):
        o_ref[:, h * HEAD_DIM:(h + 1) * HEAD_DIM] = acc_ref[h].astype(BF16)


def _sb_prompt(bias, qh, kh, vh, *, batch, seq):
    m = qh.shape[1]
    nb = seq // SB_BLOCK
    kv_spec = pl.BlockSpec((C_HEADS, seq, HEAD_DIM), lambda b, i: (0, b, 0))
    return pl.pallas_call(
        _sb_prompt_kernel,
        grid=(batch, nb),
        in_specs=[pl.BlockSpec(memory_space=pltpu.SMEM),
                  pl.BlockSpec((C_HEADS, SB_BLOCK, HEAD_DIM), lambda b, i: (0, b * nb + i, 0)), kv_spec, kv_spec],
        out_specs=pl.BlockSpec((SB_BLOCK, C_WIDTH), lambda b, i: (b * nb + i, 0)),
        out_shape=jax.ShapeDtypeStruct((m, C_WIDTH), BF16),
        scratch_shapes=[pltpu.VMEM((C_HEADS, SB_BLOCK, HEAD_DIM), F32), pltpu.VMEM((C_HEADS, SB_BLOCK, 1), F32)],
        compiler_params=pltpu.CompilerParams(dimension_semantics=("arbitrary", "arbitrary"),
                                             vmem_limit_bytes=VMEM_LIMIT),
        name="sb_prompt",
    )(bias, qh, kh, vh)


def _sb_sample_kernel(n_pages, page_size, pt_ref, bias_ref, q_ref, kn_ref, vn_ref, *refs):
    k_refs = refs[:n_pages]
    v_refs = refs[n_pages:2 * n_pages]
    o_ref = refs[2 * n_pages]
    rows = 8
    tri = _tri_strict(page_size)
    r_col = lax.broadcasted_iota(jnp.int32, (rows, 1), 0)

    def by_row(per_head):
        out = per_head[C_HEADS - 1]
        for h in reversed(range(C_HEADS - 1)):
            out = jnp.where(r_col == h, per_head[h], out)
        return out

    bias = by_row([jnp.full((rows, 1), bias_ref[h], F32) for h in range(C_HEADS)])
    lanes = [slice(h * HEAD_DIM, (h + 1) * HEAD_DIM) for h in range(C_HEADS)]
    q_h = [jnp.broadcast_to(q_ref[0, :, lanes[h]], (rows, HEAD_DIM)) for h in range(C_HEADS)]
    q_b = [x.astype(BF16) for x in q_h]

    past = n_pages * page_size
    new_is_causal = (past + 0) < (past + 0)
    z_new = by_row([jnp.sum(q_h[h] * kn_ref[0, :, lanes[h]], axis=-1, keepdims=True) for h in range(C_HEADS)]) + bias
    carry = jnp.where(new_is_causal, -_softplus(z_new), 0.0)
    w_new = jnp.where(new_is_causal, jnp.exp(-_softplus(-z_new)), 0.0)

    pages = list(reversed(range(n_pages)))
    zs = [by_row([_dot(q_b[h], k_refs[p][0, 0, h].astype(BF16)) for h in range(C_HEADS)]) + bias for p in pages]
    att, _ = _sb_chains([zs], [[None] * n_pages], [carry], tri)
    for h in range(C_HEADS):
        pv = [_dot_nt(att[0][n], v_refs[p][0, 0, h].astype(BF16)) for n, p in enumerate(pages)]
        out = w_new * vn_ref[0, :, lanes[h]] + functools.reduce(lambda a, b: a + b, pv)
        o_ref[0, :, lanes[h]] = out[h:h + 1].astype(BF16)


def _sb_sample(page_table, bias, q, k_new, v_new, cache_k, cache_v, layer):
    m = q.shape[0]
    n_pages = page_table.shape[1]
    page_size = cache_k.shape[2]
    cache_k = jnp.transpose(cache_k, (0, 1, 3, 4, 2))
    cache_v = jnp.transpose(cache_v, (0, 1, 3, 4, 2))
    pt = page_table.reshape(-1)
    row3 = lambda x: x.reshape(m, 1, C_WIDTH)
    row_spec = pl.BlockSpec((1, 1, C_WIDTH), lambda b, pt: (b, 0, 0))

    def page_spec(p):
        return pl.BlockSpec((1, 1, C_HEADS, HEAD_DIM, page_size),
                            lambda b, pt: (layer, pt[b * n_pages + p], 0, 0, 0))

    grid_spec = pltpu.PrefetchScalarGridSpec(
        num_scalar_prefetch=1,
        grid=(m,),
        in_specs=[pl.BlockSpec(memory_space=pltpu.SMEM), row_spec, row_spec, row_spec]
        + [page_spec(p) for p in range(n_pages)] + [page_spec(p) for p in range(n_pages)],
        out_specs=row_spec,
    )
    out = pl.pallas_call(
        functools.partial(_sb_sample_kernel, n_pages, page_size),
        grid_spec=grid_spec,
        out_shape=jax.ShapeDtypeStruct((m, 1, C_WIDTH), BF16),
        compiler_params=pltpu.CompilerParams(dimension_semantics=("arbitrary",), vmem_limit_bytes=VMEM_LIMIT),
        name="sb_sample",
    )(pt, bias, row3(q), row3(k_new), row3(v_new), *([cache_k] * n_pages), *([cache_v] * n_pages))
    return out.reshape(m, C_WIDTH)


def _tail_kernel(h_ref, ya_ref, yb_ref, yc_ref, pe_ref, wo_ref, fg_ref, wg_ref, wu_ref, wd_ref, pg_ref, wpg_ref,
                 wple_ref, out_ref):
    h = (h_ref[...] + _dot(ya_ref[...], wo_ref[0:A_WIDTH, :]) + _dot(yb_ref[...], wo_ref[A_WIDTH:A_WIDTH + B_WIDTH, :])
         + _dot(yc_ref[...], wo_ref[A_WIDTH + B_WIDTH:, :]))
    hn = _rms(h, fg_ref[...]).astype(BF16)
    gate = _dot(hn, wg_ref[...])
    act = (gate * jax.nn.sigmoid(gate) * _dot(hn, wu_ref[...])).astype(BF16)
    h = h + _dot(act, wd_ref[...])
    gate = jax.nn.sigmoid(_dot(_rms(h, pg_ref[...]).astype(BF16), wpg_ref[...]))
    out_ref[...] = h + _dot(pe_ref[...].astype(BF16), wple_ref[...]) * gate


def _tail(h, ya, yb, yc, pe, wo, fg, wg, wu, wd, pg, wpg, wple, *, tm):
    m, d = h.shape
    row = lambda width: pl.BlockSpec((tm, width), lambda i: (i, 0))
    weights = (wo, fg, wg, wu, wd, pg, wpg, wple)
    return pl.pallas_call(
        _tail_kernel,
        grid=(m // tm,),
        in_specs=[row(d), row(A_WIDTH), row(B_WIDTH), row(C_WIDTH), row(pe.shape[1])]
        + [_const_spec(w.shape) for w in weights],
        out_specs=row(d),
        out_shape=jax.ShapeDtypeStruct((m, d), F32),
        compiler_params=pltpu.CompilerParams(dimension_semantics=("arbitrary",), vmem_limit_bytes=VMEM_LIMIT),
        name="tail",
    )(h, ya, yb, yc, pe, *weights)


def _pad_rows(w, top, total):
    return jnp.pad(w, ((top, total - top - w.shape[0]), (0, 0)))


def kernel(x_prompt, x_sample, cache_k, cache_v, state_wkv, state_shift, page_table, p_prompt, p_sample, mix_norm, w_in, a_ln_g, a_ln_b, a_ws, a_bs, b_mu, b_w0, b_w2, b_a0, b_a2, b_g2, b_kk, b_ka, b_rk, b_ln_g, b_ln_b, c_qn, c_kn, c_bias, w_out, ffn_norm, w_gate, w_up, w_down, ple_norm, w_ple_gate, w_ple):
    depth = w_in.shape[0]
    bp, seq, d = x_prompt.shape
    bs = x_sample.shape[0]
    hp = x_prompt.reshape(bp * seq, d)
    hs = x_sample.reshape(bs, d)
    hd = lax.broadcasted_iota(jnp.int32, (B_WIDTH, B_WIDTH), 0) // HEAD_DIM
    bd = (hd == hd.T).astype(BF16)
    row = lambda x: x.reshape(1, -1)

    outs = {n: [] for n in ("kp", "vp", "wkvp", "shp", "ks", "vs", "wkvs", "shs", "avs")}
    for i in range(depth):
        w_in_b = w_in[i].astype(BF16)
        proj_common = (row(mix_norm[i]), w_in_b, row(a_ln_g[i]), row(a_ln_b[i]))
        bias_lanes = jnp.repeat(a_bs[i].T, HEAD_DIM, axis=1)
        w00_lanes = jnp.repeat(a_ws[i][:, 0, 0], HEAD_DIM).reshape(1, A_WIDTH)
        rw_params = (row(b_mu[i]), row(b_w0[i]), _pad_rows(b_w2[i], 0, LORA_WA).astype(BF16), row(b_a0[i]),
                     _pad_rows(b_a2[i], LORA_WA - b_a2.shape[1], LORA_WA).astype(BF16), b_g2[i].astype(BF16),
                     row(b_kk[i]), row(b_ka[i]), row(b_rk[i]), row(b_ln_g[i]), row(b_ln_b[i]), bd)
        tail_w = (w_out[i].astype(BF16), row(ffn_norm[i]), w_gate[i].astype(BF16), w_up[i].astype(BF16),
                  w_down[i].astype(BF16), row(ple_norm[i]), w_ple_gate[i].astype(BF16), w_ple[i].astype(BF16))

        ya, pb, qh, kh, vh, k_n, v_n = _proj(hp, *proj_common, a_ws[i], bias_lanes, row(c_qn[i]), row(c_kn[i]),
                                             sample=False, tm=512)
        yb, s_fin = _rwkv_prompt(pb, rw_params, batch=bp, seq=seq, tt=256)
        yc = _sb_prompt(c_bias[i], qh, kh, vh, batch=bp, seq=seq)
        hp = _tail(hp, ya, yb, yc, p_prompt[i].reshape(bp * seq, -1), *tail_w, tm=512)
        outs["kp"].append(k_n.reshape(bp, seq, C_HEADS, HEAD_DIM))
        outs["vp"].append(v_n.reshape(bp, seq, C_HEADS, HEAD_DIM))
        outs["wkvp"].append(s_fin)
        outs["shp"].append(pb.reshape(bp, seq, B_COLS)[:, -1])

        ya, pb, q_n, k_n, v_n, a_vn = _proj(hs, *proj_common, w00_lanes, bias_lanes[0:1], row(c_qn[i]), row(c_kn[i]),
                                            sample=True, tm=bs)
        yb, s_new = _rwkv_step(pb, state_shift[i], state_wkv[i], rw_params, bt=8)
        yc = _sb_sample(page_table, c_bias[i], q_n, k_n, v_n, cache_k, cache_v, i)
        hs = _tail(hs, ya, yb, yc, p_sample[i].reshape(bs, -1), *tail_w, tm=bs)
        outs["ks"].append(k_n.reshape(bs, 1, C_HEADS, HEAD_DIM))
        outs["vs"].append(v_n.reshape(bs, 1, C_HEADS, HEAD_DIM))
        outs["wkvs"].append(s_new)
        outs["shs"].append(pb)
        outs["avs"].append(a_vn.reshape(bs, 1, A_WIDTH))

    st = {n: jnp.stack(v) for n, v in outs.items()}
    return (hp.reshape(bp, seq, d), hs.reshape(bs, 1, d), st["kp"], st["vp"], st["wkvp"], st["shp"],
            st["ks"], st["vs"], st["wkvs"], st["shs"], st["avs"])
```

```python
import functools

import jax
import jax.numpy as jnp
from jax import lax
from jax.experimental import pallas as pl
from jax.experimental.pallas import tpu as pltpu

F32 = jnp.float32
BF16 = jnp.bfloat16

HEAD_DIM = 64
A_WIDTH = 256
A_GROUPS = 4
CHUNK = 128
B_WIDTH = 512
B_HEADS = 8
B_COLS = 1792
LORA_WA = 128
C_WIDTH = 256
C_HEADS = 4
SB_BLOCK = 128
SB_GROUP = 4
SB_SCALE = HEAD_DIM ** -0.5
RMS_EPS = 1e-6
LN_EPS = 1e-5
GN_EPS = 64e-5
RW_CHUNK = 64

VMEM_LIMIT = 56 * 1024 * 1024


def _dot(a, b):
    return jnp.dot(a, b, preferred_element_type=F32)


def _dot_nt(a, b):
    return lax.dot_general(a, b, (((1,), (1,)), ((), ())), preferred_element_type=F32)


def _dot_tn(a, b):
    return lax.dot_general(a, b, (((0,), (0,)), ((), ())), preferred_element_type=F32)


def _rms(x, g):
    return x * lax.rsqrt(jnp.mean(x * x, axis=-1, keepdims=True) + RMS_EPS) * g


def _softplus(x):
    return jnp.maximum(x, 0.0) + jnp.log1p(jnp.exp(-jnp.abs(x)))


def _const_spec(shape):
    nd = len(shape)
    return pl.BlockSpec(shape, lambda *_: (0,) * nd, pipeline_mode=pl.Buffered(1))


def _proj_kernel(sample, h_ref, g_ref, w_ref, lng_ref, lnb_ref, ws_ref, bs_ref, qn_ref, kn_ref, bd_ref, *outs):
    if sample:
        ya_ref, pb_ref, q_ref, k_ref, v_ref, av_ref = outs
    else:
        ya_ref, pb_ref, qh_ref, kh_ref, vh_ref, k_ref, v_ref = outs
    tm = h_ref.shape[0]
    hn = _rms(h_ref[...], g_ref[...]).astype(BF16)

    pc = _dot(hn, w_ref[:, 2 * A_WIDTH + B_COLS:])
    q = pc[:, :C_WIDTH]
    k = pc[:, C_WIDTH:2 * C_WIDTH]
    v = pc[:, 2 * C_WIDTH:]
    bd = bd_ref[...]
    q = q * lax.rsqrt(_head_sum(q * q, bd) * (1.0 / HEAD_DIM) + RMS_EPS) * (qn_ref[...] * SB_SCALE)
    k = k * lax.rsqrt(_head_sum(k * k, bd) * (1.0 / HEAD_DIM) + RMS_EPS) * kn_ref[...]
    k_ref[...] = k
    v_ref[...] = v
    if sample:
        q_ref[...] = q
    else:
        for dst, x in ((qh_ref, q.astype(BF16)), (kh_ref, k.astype(BF16)), (vh_ref, v.astype(BF16))):
            for h in range(C_HEADS):
                dst[h] = x[:, h * HEAD_DIM:(h + 1) * HEAD_DIM]

    pa = _dot(hn, w_ref[:, 0:2 * A_WIDTH])
    a_u = pa[:, :A_WIDTH]
    a_v = pa[:, A_WIDTH:]
    xc = a_v - jnp.mean(a_v, axis=-1, keepdims=True)
    a_vn = xc * lax.rsqrt(jnp.mean(xc * xc, axis=-1, keepdims=True) + LN_EPS) * lng_ref[...] + lnb_ref[...]
    if sample:
        av_ref[...] = a_vn
        ya_ref[...] = (a_u * (ws_ref[...] * a_vn + bs_ref[...])).astype(BF16)
    else:
        causal = lax.broadcasted_iota(jnp.int32, (CHUNK, CHUNK), 0) >= lax.broadcasted_iota(jnp.int32, (CHUNK, CHUNK), 1)
        group = lax.broadcasted_iota(jnp.int32, (CHUNK, A_WIDTH), 1) // HEAD_DIM
        w_tril = [jnp.where(causal, ws_ref[g], 0.0).astype(BF16) for g in range(A_GROUPS)]
        for c in range(tm // CHUNK):
            rows = slice(c * CHUNK, (c + 1) * CHUNK)
            v_c = a_vn[rows]
            mixed = bs_ref[...]
            for g in range(A_GROUPS):
                mixed = mixed + _dot(w_tril[g], jnp.where(group == g, v_c, 0.0).astype(BF16))
            ya_ref[rows, :] = (a_u[rows] * mixed).astype(BF16)

    pb_ref[...] = _dot(hn, w_ref[:, 2 * A_WIDTH:2 * A_WIDTH + B_COLS])


def _proj(h, g, w, lng, lnb, ws, bs, qn, kn, bd, *, sample, tm):
    m, d = h.shape
    row = lambda width: pl.BlockSpec((tm, width), lambda i: (i, 0))
    head = pl.BlockSpec((C_HEADS, tm, HEAD_DIM), lambda i: (0, i, 0))
    in_specs = [row(d), _const_spec(g.shape), _const_spec(w.shape), _const_spec(lng.shape), _const_spec(lnb.shape),
                _const_spec(ws.shape), _const_spec(bs.shape), _const_spec(qn.shape), _const_spec(kn.shape),
                _const_spec(bd.shape)]
    if sample:
        out_shape = [jax.ShapeDtypeStruct((m, A_WIDTH), BF16), jax.ShapeDtypeStruct((m, B_COLS), F32),
                     jax.ShapeDtypeStruct((m, C_WIDTH), F32), jax.ShapeDtypeStruct((m, C_WIDTH), F32),
                     jax.ShapeDtypeStruct((m, C_WIDTH), F32), jax.ShapeDtypeStruct((m, A_WIDTH), F32)]
        out_specs = [row(A_WIDTH), row(B_COLS), row(C_WIDTH), row(C_WIDTH), row(C_WIDTH), row(A_WIDTH)]
    else:
        hshape = jax.ShapeDtypeStruct((C_HEADS, m, HEAD_DIM), BF16)
        out_shape = [jax.ShapeDtypeStruct((m, A_WIDTH), BF16), jax.ShapeDtypeStruct((m, B_COLS), F32),
                     hshape, hshape, hshape,
                     jax.ShapeDtypeStruct((m, C_WIDTH), F32), jax.ShapeDtypeStruct((m, C_WIDTH), F32)]
        out_specs = [row(A_WIDTH), row(B_COLS), head, head, head, row(C_WIDTH), row(C_WIDTH)]
    return pl.pallas_call(
        functools.partial(_proj_kernel, sample),
        grid=(m // tm,),
        in_specs=in_specs,
        out_specs=out_specs,
        out_shape=out_shape,
        compiler_params=pltpu.CompilerParams(dimension_semantics=("arbitrary",), vmem_limit_bytes=VMEM_LIMIT),
        name="proj_sample" if sample else "proj_prompt",
    )(h, g, w, lng, lnb, ws, bs, qn, kn, bd)


def _head_sum(x, bd):
    return _dot(x.astype(BF16), bd)


def _rwkv_prep(x, prev, mu, w0, w2p, a0, a2p, g2, k_k, k_a, bd):
    xs = x + (prev - x) * mu
    r = xs[:, 0:B_WIDTH]
    k = xs[:, B_WIDTH:2 * B_WIDTH]
    v = xs[:, 2 * B_WIDTH:3 * B_WIDTH]
    wa = xs[:, 3 * B_WIDTH:3 * B_WIDTH + LORA_WA]
    gl = xs[:, 3 * B_WIDTH + LORA_WA:]
    w_log = -_softplus(-(w0 + _dot(jnp.tanh(wa).astype(BF16), w2p))) - 0.5
    log_decay = -jnp.exp(w_log)
    a = jax.nn.sigmoid(a0 + _dot(wa.astype(BF16), a2p))
    g = _dot(jax.nn.sigmoid(gl).astype(BF16), g2)
    kk = k * k_k
    kk = kk * lax.rsqrt(_head_sum(kk * kk, bd) + 1e-12)
    k2 = k * (1.0 + (a - 1.0) * k_a)
    return r, k2, v, kk, a, log_decay, g


def _rwkv_post(y, r, k2, v, g, r_k, ln_g, ln_b, bd):
    yc = y - _head_sum(y, bd) * (1.0 / HEAD_DIM)
    yn = yc * lax.rsqrt(_head_sum(yc * yc, bd) * (1.0 / HEAD_DIM) + GN_EPS) * ln_g + ln_b
    bonus = _head_sum(r * k2 * r_k, bd) * v
    return (yn + bonus) * g


def _bmm(a, b):
    return lax.dot_general(a, b, (((2,), (1,)), ((0,), (0,))), preferred_element_type=F32)


def _bmm_nt(a, b):
    return lax.dot_general(a, b, (((2,), (2,)), ((0,), (0,))), preferred_element_type=F32)


def _bmm_tn(a, b):
    return lax.dot_general(a, b, (((1,), (1,)), ((0,), (0,))), preferred_element_type=F32)


def _rwkv_prompt_kernel(pb_ref, mu_ref, w0_ref, w2_ref, a0_ref, a2_ref, g2_ref, kk_ref, ka_ref, rk_ref, lng_ref,
                        lnb_ref, bd_ref, y_ref, s_ref,
                        prev_ref, at_ref, rt_ref, bt_ref, kt_ref, vv_ref, pc_ref, o_ref, post_ref):
    tt = pb_ref.shape[0]
    n_chunks = tt // RW_CHUNK
    t_idx = pl.program_id(1)

    @pl.when(t_idx == 0)
    def _():
        prev_ref[...] = jnp.zeros_like(prev_ref)
        s_ref[...] = jnp.zeros_like(s_ref)

    x = pb_ref[...]
    row_id = lax.broadcasted_iota(jnp.int32, x.shape, 0)
    prev = jnp.where(row_id == 0, prev_ref[0:1, :], pltpu.roll(x, 1, axis=0))
    prev_ref[0:1, :] = x[tt - 1:tt, :]
    bd = bd_ref[...]
    r, k2, v, kk, a, log_decay, g = _rwkv_prep(x, prev, mu_ref[...], w0_ref[...], w2_ref[...], a0_ref[...],
                                               a2_ref[...], g2_ref[...], kk_ref[...], ka_ref[...], bd)

    ti = lax.broadcasted_iota(jnp.int32, (tt, tt), 0)
    si = lax.broadcasted_iota(jnp.int32, (tt, tt), 1)
    same_chunk_lower = ((ti // RW_CHUNK) == (si // RW_CHUNK)) & (si <= ti)
    cum = _split_dot_left(jnp.where(same_chunk_lower, 1.0, 0.0).astype(BF16), log_decay)
    p_in = jnp.exp(cum)
    p_ex = jnp.exp(cum - log_decay)
    p_inv = jnp.exp(-cum)
    a_t = -kk * p_ex
    b_t = kk * a * p_inv
    k_t = k2 * p_inv
    r_t = r * p_in
    for h in range(B_HEADS):
        lanes = slice(h * HEAD_DIM, (h + 1) * HEAD_DIM)
        cols = [z[:, lanes] for z in (a_t, r_t, b_t, k_t, v, p_in)]
        for c in range(n_chunks):
            rows = slice(c * RW_CHUNK, (c + 1) * RW_CHUNK)
            n = c * B_HEADS + h
            at_ref[n] = cols[0][rows].astype(BF16)
            rt_ref[n] = cols[1][rows]
            bt_ref[n] = cols[2][rows].astype(BF16)
            kt_ref[n] = cols[3][rows].astype(BF16)
            vv_ref[n] = cols[4][rows].astype(BF16)
            pc_ref[n] = cols[5][(c + 1) * RW_CHUNK - 1:(c + 1) * RW_CHUNK]
    post_ref[0] = _head_sum(r * k2 * rk_ref[...], bd) * v
    post_ref[1] = g

    ci = lax.broadcasted_iota(jnp.int32, (RW_CHUNK, RW_CHUNK), 0)
    cj = lax.broadcasted_iota(jnp.int32, (RW_CHUNK, RW_CHUNK), 1)
    ci2 = lax.broadcasted_iota(jnp.int32, (RW_CHUNK, 2 * RW_CHUNK), 0)
    cj2 = lax.broadcasted_iota(jnp.int32, (RW_CHUNK, 2 * RW_CHUNK), 1) % RW_CHUNK
    strict = (cj2 < ci2)[None]
    incl = (cj2 <= ci2)[None]
    eye = jnp.where(ci == cj, 1.0, 0.0)[None]

    at = at_ref[...]
    rt = rt_ref[...]
    bt = bt_ref[...]
    kt = kt_ref[...]
    vv = vv_ref[...]
    bk = jnp.concatenate([bt, kt], axis=1)
    aa = _bmm_nt(jnp.concatenate([at, rt.astype(BF16)], axis=1), bk)
    a_a = jnp.where(strict, aa[:, :RW_CHUNK], 0.0)
    a_r = jnp.where(incl, aa[:, RW_CHUNK:], 0.0).astype(BF16)
    a_ab = a_a[:, :, :RW_CHUNK]
    a_ak = a_a[:, :, RW_CHUNK:].astype(BF16)
    a_rb = a_r[:, :, :RW_CHUNK]
    t_inv = eye + a_ab
    n_pow = a_ab
    for _ in range(5):
        n_b = n_pow.astype(BF16)
        n_pow = _bmm(n_b, n_b)
        t_inv = t_inv + _bmm(n_pow.astype(BF16), t_inv.astype(BF16))
    t_b = t_inv.astype(BF16)
    w_m = _bmm(t_b, at).astype(BF16)
    u0 = _bmm(t_b, _bmm(a_ak, vv).astype(BF16)).astype(BF16)
    uv = jnp.concatenate([u0, vv], axis=1)
    m_p = _bmm_tn(bt, w_m).astype(BF16)
    g_t = _bmm_tn(uv, bk)
    r_p = (rt + _bmm(a_rb, w_m)).astype(BF16)
    o_loc = _bmm(a_r, uv)

    s = s_ref[0]
    for c in range(n_chunks):
        tiles = slice(c * B_HEADS, (c + 1) * B_HEADS)
        sb = s.astype(BF16)
        o_c = _bmm_nt(r_p[tiles], sb) + o_loc[tiles]
        s = (s + _bmm_nt(sb, m_p[tiles]) + g_t[tiles]) * pc_ref[tiles]
        o_c = o_c - jnp.mean(o_c, axis=-1, keepdims=True)
        o_c = o_c * lax.rsqrt(jnp.mean(o_c * o_c, axis=-1, keepdims=True) + GN_EPS)
        for h in range(B_HEADS):
            o_ref[c * RW_CHUNK:(c + 1) * RW_CHUNK, h * HEAD_DIM:(h + 1) * HEAD_DIM] = o_c[h]
    s_ref[0] = s

    yn = o_ref[...] * lng_ref[...] + lnb_ref[...]
    y_ref[...] = ((yn + post_ref[0]) * post_ref[1]).astype(BF16)


def _split_dot_left(m, x):
    hi = x.astype(BF16)
    lo = (x - hi.astype(F32)).astype(BF16)
    return _dot(m, hi) + _dot(m, lo)


def _rwkv_prompt(pb, params, *, batch, seq, tt):
    m = pb.shape[0]
    nt = seq // tt
    in_specs = [pl.BlockSpec((tt, B_COLS), lambda b, t: (b * nt + t, 0))] + [_const_spec(p.shape) for p in params]
    n_tiles = B_HEADS * tt // RW_CHUNK
    head_scratch = lambda dt: pltpu.VMEM((n_tiles, RW_CHUNK, HEAD_DIM), dt)
    return pl.pallas_call(
        _rwkv_prompt_kernel,
        grid=(batch, nt),
        in_specs=in_specs,
        out_specs=[pl.BlockSpec((tt, B_WIDTH), lambda b, t: (b * nt + t, 0)),
                   pl.BlockSpec((1, B_HEADS, HEAD_DIM, HEAD_DIM), lambda b, t: (b, 0, 0, 0))],
        out_shape=[jax.ShapeDtypeStruct((m, B_WIDTH), BF16),
                   jax.ShapeDtypeStruct((batch, B_HEADS, HEAD_DIM, HEAD_DIM), F32)],
        scratch_shapes=[pltpu.VMEM((8, B_COLS), F32), head_scratch(BF16), head_scratch(F32), head_scratch(BF16),
                        head_scratch(BF16), head_scratch(BF16), pltpu.VMEM((n_tiles, 1, HEAD_DIM), F32),
                        pltpu.VMEM((tt, B_WIDTH), F32), pltpu.VMEM((2, tt, B_WIDTH), F32)],
        compiler_params=pltpu.CompilerParams(dimension_semantics=("arbitrary", "arbitrary"),
                                             vmem_limit_bytes=VMEM_LIMIT),
        name="rwkv_prompt",
    )(pb, *params)


def _rwkv_step_kernel(pb_ref, sh_ref, s_ref, mu_ref, w0_ref, w2_ref, a0_ref, a2_ref, g2_ref, kk_ref, ka_ref, rk_ref,
                      lng_ref, lnb_ref, bd_ref, y_ref, so_ref, vec_ref, o_ref):
    bt = pb_ref.shape[0]
    bd = bd_ref[...]
    r, k2, v, kk, a, log_decay, g = _rwkv_prep(pb_ref[...], sh_ref[...], mu_ref[...], w0_ref[...], w2_ref[...],
                                               a0_ref[...], a2_ref[...], g2_ref[...], kk_ref[...], ka_ref[...], bd)
    vec_ref[0] = r
    vec_ref[1] = jnp.exp(log_decay)
    vec_ref[2] = k2
    vec_ref[3] = v
    vec_ref[4] = kk
    vec_ref[5] = kk * a
    eye = lax.broadcasted_iota(jnp.int32, (HEAD_DIM, HEAD_DIM), 0) == lax.broadcasted_iota(jnp.int32, (HEAD_DIM, HEAD_DIM), 1)
    pairs = [(b, h) for b in range(bt) for h in range(B_HEADS)]
    row = lambda i, b, h: vec_ref[i, b:b + 1, h * HEAD_DIM:(h + 1) * HEAD_DIM]
    s_kk, v_c, y_c = {}, {}, {}
    for b, h in pairs:
        s_kk[b, h] = jnp.sum(s_ref[b, h] * row(4, b, h), axis=1, keepdims=True)
        v_c[b, h] = jnp.sum(jnp.where(eye, row(3, b, h), 0.0), axis=1, keepdims=True)
    for b, h in pairs:
        s_new = s_ref[b, h] * row(1, b, h) - s_kk[b, h] * row(5, b, h) + v_c[b, h] * row(2, b, h)
        so_ref[b, h] = s_new
        y_c[b, h] = jnp.sum(s_new * row(0, b, h), axis=1, keepdims=True)
    for b, h in pairs:
        o_ref[b:b + 1, h * HEAD_DIM:(h + 1) * HEAD_DIM] = jnp.sum(jnp.where(eye, y_c[b, h], 0.0), axis=0, keepdims=True)
    y_ref[...] = _rwkv_post(o_ref[...], r, k2, v, g, rk_ref[...], lng_ref[...], lnb_ref[...], bd).astype(BF16)


def _rwkv_step(pb, shift, state, params, *, bt):
    m = pb.shape[0]
    st_spec = pl.BlockSpec((bt, B_HEADS, HEAD_DIM, HEAD_DIM), lambda i: (i, 0, 0, 0))
    in_specs = [pl.BlockSpec((bt, B_COLS), lambda i: (i, 0)), pl.BlockSpec((bt, B_COLS), lambda i: (i, 0)), st_spec]
    in_specs += [_const_spec(p.shape) for p in params]
    return pl.pallas_call(
        _rwkv_step_kernel,
        grid=(m // bt,),
        in_specs=in_specs,
        out_specs=[pl.BlockSpec((bt, B_WIDTH), lambda i: (i, 0)), st_spec],
        out_shape=[jax.ShapeDtypeStruct((m, B_WIDTH), BF16), jax.ShapeDtypeStruct(state.shape, F32)],
        scratch_shapes=[pltpu.VMEM((6, bt, B_WIDTH), F32), pltpu.VMEM((bt, B_WIDTH), F32)],
        compiler_params=pltpu.CompilerParams(dimension_semantics=("arbitrary",), vmem_limit_bytes=VMEM_LIMIT),
        name="rwkv_sample",
    )(pb, shift, state, *params)


def _sb_chains(z_chains, mask_chains, carries, tri):
    units = [(c, n) for c in range(len(z_chains)) for n in range(len(z_chains[c]))]
    log_beta, log_keep, after, att = {}, {}, {}, {}
    for c, n in units:
        z = z_chains[c][n]
        log_beta[c, n] = jnp.minimum(z, 0.0) - jnp.log(1.0 + jnp.exp(-jnp.abs(z)))
        keep = log_beta[c, n] - z
        log_keep[c, n] = keep if mask_chains[c][n] is None else jnp.where(mask_chains[c][n], keep, 0.0)
    for u in units:
        after[u] = _dot(log_keep[u].astype(BF16), tri)
    carries = list(carries)
    keys = tri.shape[0]
    for c, n in units:
        w = jnp.exp(log_beta[c, n] + after[c, n][:, :keys] + carries[c])
        att[c, n] = (w if mask_chains[c][n] is None else jnp.where(mask_chains[c][n], w, 0.0)).astype(BF16)
        carries[c] = carries[c] + after[c, n][:, keys:]
    return [[att[c, n] for n in range(len(z_chains[c]))] for c in range(len(z_chains))], carries


def _suffix_and_total(n):
    later = lax.broadcasted_iota(jnp.int32, (n, 2 * n), 0) > lax.broadcasted_iota(jnp.int32, (n, 2 * n), 1)
    whole = lax.broadcasted_iota(jnp.int32, (n, 2 * n), 1) >= n
    return jnp.where(later | whole, 1.0, 0.0).astype(BF16)


def _sb_prompt_kernel(bias_ref, q_ref, k_ref, v_ref, o_ref, acc_ref, carry_ref):
    blk = q_ref.shape[1]
    i = pl.program_id(1)
    tri = _suffix_and_total(blk)
    causal = lax.broadcasted_iota(jnp.int32, (blk, blk), 1) < lax.broadcasted_iota(jnp.int32, (blk, blk), 0)
    acc_ref[...] = jnp.zeros_like(acc_ref)
    carry_ref[...] = jnp.zeros_like(carry_ref)

    def process(blocks, masks):
        offs = [pl.multiple_of(j * blk, blk) for j in blocks]
        zs = [[_dot_nt(q_ref[h], k_ref[h, pl.ds(off, blk), :]) + bias_ref[h] for off in offs] for h in range(C_HEADS)]
        att, carry = _sb_chains(zs, [masks] * C_HEADS, [carry_ref[h] for h in range(C_HEADS)], tri)
        pv = [[_dot(att[h][n], v_ref[h, pl.ds(off, blk), :]) for n, off in enumerate(offs)] for h in range(C_HEADS)]
        for h in range(C_HEADS):
            acc_ref[h] += functools.reduce(lambda a, b: a + b, pv[h])
            carry_ref[h] = carry[h]

    for rem in range(SB_GROUP):
        @pl.when(i % SB_GROUP == rem)
        def _():
            process([i - n for n in range(rem + 1)], [causal] + [None] * rem)

    def group(p, _):
        first = i - i % SB_GROUP - 1 - SB_GROUP * p
        process([first - n for n in range(SB_GROUP)], [None] * SB_GROUP)
        return 0

    lax.fori_loop(0, i // SB_GROUP, group, 0)

    for h in range(C_HEADS):
        o_ref[:, h * HEAD_DIM:(h + 1) * HEAD_DIM] = acc_ref[h].astype(BF16)


def _sb_prompt(bias, qh, kh, vh, *, batch, seq):
    m = qh.shape[1]
    nb = seq // SB_BLOCK
    kv_spec = pl.BlockSpec((C_HEADS, seq, HEAD_DIM), lambda b, i: (0, b, 0))
    return pl.pallas_call(
        _sb_prompt_kernel,
        grid=(batch, nb),
        in_specs=[pl.BlockSpec(memory_space=pltpu.SMEM),
                  pl.BlockSpec((C_HEADS, SB_BLOCK, HEAD_DIM), lambda b, i: (0, b * nb + i, 0)), kv_spec, kv_spec],
        out_specs=pl.BlockSpec((SB_BLOCK, C_WIDTH), lambda b, i: (b * nb + i, 0)),
        out_shape=jax.ShapeDtypeStruct((m, C_WIDTH), BF16),
        scratch_shapes=[pltpu.VMEM((C_HEADS, SB_BLOCK, HEAD_DIM), F32), pltpu.VMEM((C_HEADS, SB_BLOCK, SB_BLOCK), F32)],
        compiler_params=pltpu.CompilerParams(dimension_semantics=("arbitrary", "arbitrary"),
                                             vmem_limit_bytes=VMEM_LIMIT),
        name="sb_prompt",
    )(bias, qh, kh, vh)


def _sb_sample_kernel(n_seq, n_pages, page_size, pt_ref, bias_ref, q_ref, kn_ref, vn_ref, *refs):
    n_blocks = n_seq * n_pages
    k_refs = refs[:n_blocks]
    v_refs = refs[n_blocks:2 * n_blocks]
    o_ref = refs[2 * n_blocks]
    rows = 8
    tri = _suffix_and_total(page_size)
    r_col = lax.broadcasted_iota(jnp.int32, (rows, 1), 0)

    def by_row(per_head):
        out = per_head[C_HEADS - 1]
        for h in reversed(range(C_HEADS - 1)):
            out = jnp.where(r_col == h, per_head[h], out)
        return out

    bias = by_row([jnp.full((rows, 1), bias_ref[h], F32) for h in range(C_HEADS)])
    lanes = [slice(h * HEAD_DIM, (h + 1) * HEAD_DIM) for h in range(C_HEADS)]
    past = n_pages * page_size
    new_is_causal = (past + 0) < (past + 0)
    pages = list(reversed(range(n_pages)))

    z_chains, carries, w_new = [], [], []
    for s in range(n_seq):
        q_h = [jnp.broadcast_to(q_ref[s, :, lanes[h]], (rows, HEAD_DIM)) for h in range(C_HEADS)]
        q_b = [x.astype(BF16) for x in q_h]
        z_new = by_row([jnp.sum(q_h[h] * kn_ref[s, :, lanes[h]], axis=-1, keepdims=True)
                        for h in range(C_HEADS)]) + bias
        carries.append(jnp.broadcast_to(jnp.where(new_is_causal, -_softplus(z_new), 0.0), (rows, page_size)))
        w_new.append(jnp.where(new_is_causal, jnp.exp(-_softplus(-z_new)), 0.0))
        z_chains.append([by_row([_dot(q_b[h], k_refs[s * n_pages + p][0, 0, h].astype(BF16))
                                 for h in range(C_HEADS)]) + bias for p in pages])
    att, _ = _sb_chains(z_chains, [[None] * n_pages] * n_seq, carries, tri)
    for s in range(n_seq):
        for h in range(C_HEADS):
            pv = [_dot_nt(att[s][n], v_refs[s * n_pages + p][0, 0, h].astype(BF16)) for n, p in enumerate(pages)]
            out = w_new[s] * vn_ref[s, :, lanes[h]] + functools.reduce(lambda a, b: a + b, pv)
            o_ref[s, :, lanes[h]] = out[h:h + 1].astype(BF16)


def _sb_sample(page_table, bias, q, k_new, v_new, cache_k, cache_v, layer, *, n_seq):
    m = q.shape[0]
    n_pages = page_table.shape[1]
    page_size = cache_k.shape[2]
    cache_k = jnp.transpose(cache_k, (0, 1, 3, 4, 2))
    cache_v = jnp.transpose(cache_v, (0, 1, 3, 4, 2))
    pt = page_table.reshape(-1)
    row3 = lambda x: x.reshape(m, 1, C_WIDTH)
    row_spec = pl.BlockSpec((n_seq, 1, C_WIDTH), lambda b, pt: (b, 0, 0))

    def page_spec(s, p):
        return pl.BlockSpec((1, 1, C_HEADS, HEAD_DIM, page_size),
                            lambda b, pt: (layer, pt[(b * n_seq + s) * n_pages + p], 0, 0, 0))

    page_specs = [page_spec(s, p) for s in range(n_seq) for p in range(n_pages)]
    grid_spec = pltpu.PrefetchScalarGridSpec(
        num_scalar_prefetch=1,
        grid=(m // n_seq,),
        in_specs=[pl.BlockSpec(memory_space=pltpu.SMEM), row_spec, row_spec, row_spec] + page_specs + page_specs,
        out_specs=row_spec,
    )
    n_blocks = n_seq * n_pages
    out = pl.pallas_call(
        functools.partial(_sb_sample_kernel, n_seq, n_pages, page_size),
        grid_spec=grid_spec,
        out_shape=jax.ShapeDtypeStruct((m, 1, C_WIDTH), BF16),
        compiler_params=pltpu.CompilerParams(dimension_semantics=("arbitrary",), vmem_limit_bytes=VMEM_LIMIT),
        name="sb_sample",
    )(pt, bias, row3(q), row3(k_new), row3(v_new), *([cache_k] * n_blocks), *([cache_v] * n_blocks))
    return out.reshape(m, C_WIDTH)


def _tail_kernel(h_ref, ya_ref, yb_ref, yc_ref, pe_ref, wo_ref, fg_ref, wg_ref, wu_ref, wd_ref, pg_ref, wpg_ref,
                 wple_ref, out_ref):
    h = (h_ref[...] + _dot(ya_ref[...], wo_ref[0:A_WIDTH, :]) + _dot(yb_ref[...], wo_ref[A_WIDTH:A_WIDTH + B_WIDTH, :])
         + _dot(yc_ref[...], wo_ref[A_WIDTH + B_WIDTH:, :]))
    hn = _rms(h, fg_ref[...]).astype(BF16)
    gate = _dot(hn, wg_ref[...])
    act = (gate * jax.nn.sigmoid(gate) * _dot(hn, wu_ref[...])).astype(BF16)
    h = h + _dot(act, wd_ref[...])
    gate = jax.nn.sigmoid(_dot(_rms(h, pg_ref[...]).astype(BF16), wpg_ref[...]))
    out_ref[...] = h + _dot(pe_ref[...].astype(BF16), wple_ref[...]) * gate


def _tail(h, ya, yb, yc, pe, wo, fg, wg, wu, wd, pg, wpg, wple, *, tm):
    m, d = h.shape
    row = lambda width: pl.BlockSpec((tm, width), lambda i: (i, 0))
    weights = (wo, fg, wg, wu, wd, pg, wpg, wple)
    return pl.pallas_call(
        _tail_kernel,
        grid=(m // tm,),
        in_specs=[row(d), row(A_WIDTH), row(B_WIDTH), row(C_WIDTH), row(pe.shape[1])]
        + [_const_spec(w.shape) for w in weights],
        out_specs=row(d),
        out_shape=jax.ShapeDtypeStruct((m, d), F32),
        compiler_params=pltpu.CompilerParams(dimension_semantics=("arbitrary",), vmem_limit_bytes=VMEM_LIMIT),
        name="tail",
    )(h, ya, yb, yc, pe, *weights)


def _pad_rows(w, top, total):
    return jnp.pad(w, ((top, total - top - w.shape[0]), (0, 0)))


def kernel(x_prompt, x_sample, cache_k, cache_v, state_wkv, state_shift, page_table, p_prompt, p_sample, mix_norm, w_in, a_ln_g, a_ln_b, a_ws, a_bs, b_mu, b_w0, b_w2, b_a0, b_a2, b_g2, b_kk, b_ka, b_rk, b_ln_g, b_ln_b, c_qn, c_kn, c_bias, w_out, ffn_norm, w_gate, w_up, w_down, ple_norm, w_ple_gate, w_ple):
    depth = w_in.shape[0]
    bp, seq, d = x_prompt.shape
    bs = x_sample.shape[0]
    hp = x_prompt.reshape(bp * seq, d)
    hs = x_sample.reshape(bs, d)
    hd = lax.broadcasted_iota(jnp.int32, (B_WIDTH, B_WIDTH), 0) // HEAD_DIM
    bd = (hd == hd.T).astype(BF16)
    row = lambda x: x.reshape(1, -1)

    outs = {n: [] for n in ("kp", "vp", "wkvp", "shp", "ks", "vs", "wkvs", "shs", "avs")}
    for i in range(depth):
        w_in_b = w_in[i].astype(BF16)
        proj_common = (row(mix_norm[i]), w_in_b, row(a_ln_g[i]), row(a_ln_b[i]))
        bias_lanes = jnp.repeat(a_bs[i].T, HEAD_DIM, axis=1)
        w00_lanes = jnp.repeat(a_ws[i][:, 0, 0], HEAD_DIM).reshape(1, A_WIDTH)
        rw_params = (row(b_mu[i]), row(b_w0[i]), _pad_rows(b_w2[i], 0, LORA_WA).astype(BF16), row(b_a0[i]),
                     _pad_rows(b_a2[i], LORA_WA - b_a2.shape[1], LORA_WA).astype(BF16), b_g2[i].astype(BF16),
                     row(b_kk[i]), row(b_ka[i]), row(b_rk[i]), row(b_ln_g[i]), row(b_ln_b[i]), bd)
        tail_w = (w_out[i].astype(BF16), row(ffn_norm[i]), w_gate[i].astype(BF16), w_up[i].astype(BF16),
                  w_down[i].astype(BF16), row(ple_norm[i]), w_ple_gate[i].astype(BF16), w_ple[i].astype(BF16))

        qk_norm = (row(jnp.tile(c_qn[i], C_HEADS)), row(jnp.tile(c_kn[i], C_HEADS)), bd[:C_WIDTH, :C_WIDTH])
        ya, pb, qh, kh, vh, k_n, v_n = _proj(hp, *proj_common, a_ws[i], bias_lanes, *qk_norm, sample=False, tm=512)
        yb, s_fin = _rwkv_prompt(pb, rw_params, batch=bp, seq=seq, tt=256)
        yc = _sb_prompt(c_bias[i], qh, kh, vh, batch=bp, seq=seq)
        hp = _tail(hp, ya, yb, yc, p_prompt[i].reshape(bp * seq, -1), *tail_w, tm=512)
        outs["kp"].append(k_n.reshape(bp, seq, C_HEADS, HEAD_DIM))
        outs["vp"].append(v_n.reshape(bp, seq, C_HEADS, HEAD_DIM))
        outs["wkvp"].append(s_fin)
        outs["shp"].append(pb.reshape(bp, seq, B_COLS)[:, -1])

        ya, pb, q_n, k_n, v_n, a_vn = _proj(hs, *proj_common, w00_lanes, bias_lanes[0:1], *qk_norm, sample=True, tm=bs)
        yb, s_new = _rwkv_step(pb, state_shift[i], state_wkv[i], rw_params, bt=8)
        yc = _sb_sample(page_table, c_bias[i], q_n, k_n, v_n, cache_k, cache_v, i, n_seq=2)
        hs = _tail(hs, ya, yb, yc, p_sample[i].reshape(bs, -1), *tail_w, tm=bs)
        outs["ks"].append(k_n.reshape(bs, 1, C_HEADS, HEAD_DIM))
        outs["vs"].append(v_n.reshape(bs, 1, C_HEADS, HEAD_DIM))
        outs["wkvs"].append(s_new)
        outs["shs"].append(pb)
        outs["avs"].append(a_vn.reshape(bs, 1, A_WIDTH))

    st = {n: jnp.stack(v) for n, v in outs.items()}
    return (hp.reshape(bp, seq, d), hs.reshape(bs, 1, d), st["kp"], st["vp"], st["wkvp"], st["shp"],
            st["ks"], st["vs"], st["wkvs"], st["shs"], st["avs"])
```

```python
import functools

import jax
import jax.numpy as jnp
from jax import lax
from jax.experimental import pallas as pl
from jax.experimental.pallas import tpu as pltpu

F32 = jnp.float32
BF16 = jnp.bfloat16

HEAD_DIM = 64
A_WIDTH = 256
A_GROUPS = 4
CHUNK = 128
B_WIDTH = 512
B_HEADS = 8
B_COLS = 1792
LORA_WA = 128
C_WIDTH = 256
C_HEADS = 4
SB_BLOCK = 128
SB_GROUP = 4
SB_SCALE = HEAD_DIM ** -0.5
RMS_EPS = 1e-6
LN_EPS = 1e-5
GN_EPS = 64e-5
RW_CHUNK = 64

VMEM_LIMIT = 56 * 1024 * 1024


def _dot(a, b):
    return jnp.dot(a, b, preferred_element_type=F32)


def _dot_nt(a, b):
    return lax.dot_general(a, b, (((1,), (1,)), ((), ())), preferred_element_type=F32)


def _dot_tn(a, b):
    return lax.dot_general(a, b, (((0,), (0,)), ((), ())), preferred_element_type=F32)


def _rms(x, g):
    return x * lax.rsqrt(jnp.mean(x * x, axis=-1, keepdims=True) + RMS_EPS) * g


def _softplus(x):
    return jnp.maximum(x, 0.0) + jnp.log1p(jnp.exp(-jnp.abs(x)))


def _const_spec(shape):
    nd = len(shape)
    return pl.BlockSpec(shape, lambda *_: (0,) * nd, pipeline_mode=pl.Buffered(1))


def _proj_kernel(sample, h_ref, g_ref, w_ref, lng_ref, lnb_ref, ws_ref, bs_ref, qn_ref, kn_ref, bd_ref, *rest):
    if sample:
        ya_ref, pb_ref, q_ref, k_ref, v_ref, av_ref = rest
    else:
        ya_ref, pb_ref, qh_ref, kh_ref, vh_ref, kt_ref, vt_ref = rest[-7:]
    tm = h_ref.shape[0]
    hn = _rms(h_ref[...], g_ref[...]).astype(BF16)

    pc = _dot(hn, w_ref[:, 2 * A_WIDTH + B_COLS:])
    q = pc[:, :C_WIDTH]
    k = pc[:, C_WIDTH:2 * C_WIDTH]
    v = pc[:, 2 * C_WIDTH:]
    bd = bd_ref[...]
    q = q * lax.rsqrt(_head_sum(q * q, bd) * (1.0 / HEAD_DIM) + RMS_EPS) * (qn_ref[...] * SB_SCALE)
    k = k * lax.rsqrt(_head_sum(k * k, bd) * (1.0 / HEAD_DIM) + RMS_EPS) * kn_ref[...]
    if sample:
        q_ref[...] = q
        k_ref[...] = k
        v_ref[...] = v
    else:
        for dst, x in ((qh_ref, q.astype(BF16)), (kh_ref, k.astype(BF16)), (vh_ref, v.astype(BF16))):
            for h in range(C_HEADS):
                dst[h] = x[:, h * HEAD_DIM:(h + 1) * HEAD_DIM]
        for dst, x in ((kt_ref, k.T), (vt_ref, v.T)):
            for h in range(C_HEADS):
                dst[0, 0, h] = x[h * HEAD_DIM:(h + 1) * HEAD_DIM, :]

    pa = _dot(hn, w_ref[:, 0:2 * A_WIDTH])
    a_u = pa[:, :A_WIDTH]
    a_v = pa[:, A_WIDTH:]
    xc = a_v - jnp.mean(a_v, axis=-1, keepdims=True)
    a_vn = xc * lax.rsqrt(jnp.mean(xc * xc, axis=-1, keepdims=True) + LN_EPS) * lng_ref[...] + lnb_ref[...]
    if sample:
        av_ref[...] = a_vn
        ya_ref[...] = (a_u * (ws_ref[...] * a_vn + bs_ref[...])).astype(BF16)
    else:
        causal = lax.broadcasted_iota(jnp.int32, (CHUNK, CHUNK), 0) >= lax.broadcasted_iota(jnp.int32, (CHUNK, CHUNK), 1)
        group = lax.broadcasted_iota(jnp.int32, (CHUNK, A_WIDTH), 1) // HEAD_DIM
        w_tril = [jnp.where(causal, ws_ref[g], 0.0).astype(BF16) for g in range(A_GROUPS)]
        for c in range(tm // CHUNK):
            rows = slice(c * CHUNK, (c + 1) * CHUNK)
            v_c = a_vn[rows]
            mixed = bs_ref[...]
            for g in range(A_GROUPS):
                mixed = mixed + _dot(w_tril[g], jnp.where(group == g, v_c, 0.0).astype(BF16))
            ya_ref[rows, :] = (a_u[rows] * mixed).astype(BF16)

    pb_ref[...] = _dot(hn, w_ref[:, 2 * A_WIDTH:2 * A_WIDTH + B_COLS])


def _proj(h, g, w, lng, lnb, ws, bs, qn, kn, bd, *, sample, tm, layer=0, seq=None, kv_shape=None, kv_bufs=None):
    m, d = h.shape
    row = lambda width: pl.BlockSpec((tm, width), lambda i: (i, 0))
    head = pl.BlockSpec((C_HEADS, tm, HEAD_DIM), lambda i: (0, i, 0))
    in_specs = [row(d), _const_spec(g.shape), _const_spec(w.shape), _const_spec(lng.shape), _const_spec(lnb.shape),
                _const_spec(ws.shape), _const_spec(bs.shape), _const_spec(qn.shape), _const_spec(kn.shape),
                _const_spec(bd.shape)]
    args = [h, g, w, lng, lnb, ws, bs, qn, kn, bd]
    aliases = {}
    if sample:
        out_shape = [jax.ShapeDtypeStruct((m, A_WIDTH), BF16), jax.ShapeDtypeStruct((m, B_COLS), F32),
                     jax.ShapeDtypeStruct((m, C_WIDTH), F32), jax.ShapeDtypeStruct((m, C_WIDTH), F32),
                     jax.ShapeDtypeStruct((m, C_WIDTH), F32), jax.ShapeDtypeStruct((m, A_WIDTH), F32)]
        out_specs = [row(A_WIDTH), row(B_COLS), row(C_WIDTH), row(C_WIDTH), row(C_WIDTH), row(A_WIDTH)]
    else:
        tiles = seq // tm
        hshape = jax.ShapeDtypeStruct((C_HEADS, m, HEAD_DIM), BF16)
        kv = jax.ShapeDtypeStruct(kv_shape, F32)
        kv_spec = pl.BlockSpec((1, 1, C_HEADS, HEAD_DIM, tm), lambda i: (layer, i // tiles, 0, 0, i % tiles))
        out_shape = [jax.ShapeDtypeStruct((m, A_WIDTH), BF16), jax.ShapeDtypeStruct((m, B_COLS), F32),
                     hshape, hshape, hshape, kv, kv]
        out_specs = [row(A_WIDTH), row(B_COLS), head, head, head, kv_spec, kv_spec]
        if kv_bufs is not None:
            in_specs += [pl.BlockSpec(memory_space=pl.ANY)] * 2
            aliases = {len(args): 5, len(args) + 1: 6}
            args += list(kv_bufs)
    return pl.pallas_call(
        functools.partial(_proj_kernel, sample),
        grid=(m // tm,),
        in_specs=in_specs,
        out_specs=out_specs,
        out_shape=out_shape,
        input_output_aliases=aliases,
        compiler_params=pltpu.CompilerParams(dimension_semantics=("arbitrary",), vmem_limit_bytes=VMEM_LIMIT),
        name="proj_sample" if sample else "proj_prompt",
    )(*args)


def _head_sum(x, bd):
    return _dot(x.astype(BF16), bd)


def _rwkv_prep(x, prev, mu, w0, w2p, a0, a2p, g2, k_k, k_a, bd):
    xs = x + (prev - x) * mu
    r = xs[:, 0:B_WIDTH]
    k = xs[:, B_WIDTH:2 * B_WIDTH]
    v = xs[:, 2 * B_WIDTH:3 * B_WIDTH]
    wa = xs[:, 3 * B_WIDTH:3 * B_WIDTH + LORA_WA]
    gl = xs[:, 3 * B_WIDTH + LORA_WA:]
    w_log = -_softplus(-(w0 + _dot(jnp.tanh(wa).astype(BF16), w2p))) - 0.5
    log_decay = -jnp.exp(w_log)
    a = jax.nn.sigmoid(a0 + _dot(wa.astype(BF16), a2p))
    g = _dot(jax.nn.sigmoid(gl).astype(BF16), g2)
    kk = k * k_k
    kk = kk * lax.rsqrt(_head_sum(kk * kk, bd) + 1e-12)
    k2 = k * (1.0 + (a - 1.0) * k_a)
    return r, k2, v, kk, a, log_decay, g


def _rwkv_post(y, r, k2, v, g, r_k, ln_g, ln_b, bd):
    yc = y - _head_sum(y, bd) * (1.0 / HEAD_DIM)
    yn = yc * lax.rsqrt(_head_sum(yc * yc, bd) * (1.0 / HEAD_DIM) + GN_EPS) * ln_g + ln_b
    bonus = _head_sum(r * k2 * r_k, bd) * v
    return (yn + bonus) * g


def _bmm(a, b):
    return lax.dot_general(a, b, (((2,), (1,)), ((0,), (0,))), preferred_element_type=F32)


def _bmm_nt(a, b):
    return lax.dot_general(a, b, (((2,), (2,)), ((0,), (0,))), preferred_element_type=F32)


def _bmm_tn(a, b):
    return lax.dot_general(a, b, (((1,), (1,)), ((0,), (0,))), preferred_element_type=F32)


def _rwkv_prompt_kernel(pb_ref, mu_ref, w0_ref, w2_ref, a0_ref, a2_ref, g2_ref, kk_ref, ka_ref, rk_ref, lng_ref,
                        lnb_ref, bd_ref, y_ref, s_ref,
                        prev_ref, at_ref, rt_ref, bt_ref, kt_ref, vv_ref, pc_ref, o_ref, post_ref):
    tt = pb_ref.shape[0]
    n_chunks = tt // RW_CHUNK
    t_idx = pl.program_id(1)

    @pl.when(t_idx == 0)
    def _():
        prev_ref[...] = jnp.zeros_like(prev_ref)
        s_ref[...] = jnp.zeros_like(s_ref)

    x = pb_ref[...]
    row_id = lax.broadcasted_iota(jnp.int32, x.shape, 0)
    prev = jnp.where(row_id == 0, prev_ref[0:1, :], pltpu.roll(x, 1, axis=0))
    prev_ref[0:1, :] = x[tt - 1:tt, :]
    bd = bd_ref[...]
    r, k2, v, kk, a, log_decay, g = _rwkv_prep(x, prev, mu_ref[...], w0_ref[...], w2_ref[...], a0_ref[...],
                                               a2_ref[...], g2_ref[...], kk_ref[...], ka_ref[...], bd)

    ti = lax.broadcasted_iota(jnp.int32, (tt, tt), 0)
    si = lax.broadcasted_iota(jnp.int32, (tt, tt), 1)
    same_chunk_lower = ((ti // RW_CHUNK) == (si // RW_CHUNK)) & (si <= ti)
    cum = _split_dot_left(jnp.where(same_chunk_lower, 1.0, 0.0).astype(BF16), log_decay)
    p_in = jnp.exp(cum)
    p_ex = jnp.exp(cum - log_decay)
    p_inv = jnp.exp(-cum)
    a_t = -kk * p_ex
    b_t = kk * a * p_inv
    k_t = k2 * p_inv
    r_t = r * p_in
    for h in range(B_HEADS):
        lanes = slice(h * HEAD_DIM, (h + 1) * HEAD_DIM)
        cols = [z[:, lanes] for z in (a_t, r_t, b_t, k_t, v, p_in)]
        for c in range(n_chunks):
            rows = slice(c * RW_CHUNK, (c + 1) * RW_CHUNK)
            n = c * B_HEADS + h
            at_ref[n] = cols[0][rows].astype(BF16)
            rt_ref[n] = cols[1][rows]
            bt_ref[n] = cols[2][rows].astype(BF16)
            kt_ref[n] = cols[3][rows].astype(BF16)
            vv_ref[n] = cols[4][rows].astype(BF16)
            pc_ref[n] = cols[5][(c + 1) * RW_CHUNK - 1:(c + 1) * RW_CHUNK]
    post_ref[0] = _head_sum(r * k2 * rk_ref[...], bd) * v
    post_ref[1] = g

    ci = lax.broadcasted_iota(jnp.int32, (RW_CHUNK, RW_CHUNK), 0)
    cj = lax.broadcasted_iota(jnp.int32, (RW_CHUNK, RW_CHUNK), 1)
    ci2 = lax.broadcasted_iota(jnp.int32, (RW_CHUNK, 2 * RW_CHUNK), 0)
    cj2 = lax.broadcasted_iota(jnp.int32, (RW_CHUNK, 2 * RW_CHUNK), 1) % RW_CHUNK
    strict = (cj2 < ci2)[None]
    incl = (cj2 <= ci2)[None]
    eye = jnp.where(ci == cj, 1.0, 0.0)[None]

    at = at_ref[...]
    rt = rt_ref[...]
    bt = bt_ref[...]
    kt = kt_ref[...]
    vv = vv_ref[...]
    bk = jnp.concatenate([bt, kt], axis=1)
    aa = _bmm_nt(jnp.concatenate([at, rt.astype(BF16)], axis=1), bk)
    a_a = jnp.where(strict, aa[:, :RW_CHUNK], 0.0)
    a_r = jnp.where(incl, aa[:, RW_CHUNK:], 0.0).astype(BF16)
    a_ab = a_a[:, :, :RW_CHUNK]
    a_ak = a_a[:, :, RW_CHUNK:].astype(BF16)
    a_rb = a_r[:, :, :RW_CHUNK]
    t_inv = eye + a_ab
    n_pow = a_ab
    for _ in range(5):
        n_b = n_pow.astype(BF16)
        n_pow = _bmm(n_b, n_b)
        t_inv = t_inv + _bmm(n_pow.astype(BF16), t_inv.astype(BF16))
    t_b = t_inv.astype(BF16)
    w_m = _bmm(t_b, at).astype(BF16)
    u0 = _bmm(t_b, _bmm(a_ak, vv).astype(BF16)).astype(BF16)
    uv = jnp.concatenate([u0, vv], axis=1)
    m_p = _bmm_tn(bt, w_m).astype(BF16)
    g_t = _bmm_tn(uv, bk)
    r_p = (rt + _bmm(a_rb, w_m)).astype(BF16)
    o_loc = _bmm(a_r, uv)

    s = s_ref[0]
    for c in range(n_chunks):
        tiles = slice(c * B_HEADS, (c + 1) * B_HEADS)
        sb = s.astype(BF16)
        o_c = _bmm_nt(r_p[tiles], sb) + o_loc[tiles]
        s = (s + _bmm_nt(sb, m_p[tiles]) + g_t[tiles]) * pc_ref[tiles]
        o_c = o_c - jnp.mean(o_c, axis=-1, keepdims=True)
        o_c = o_c * lax.rsqrt(jnp.mean(o_c * o_c, axis=-1, keepdims=True) + GN_EPS)
        for h in range(B_HEADS):
            o_ref[c * RW_CHUNK:(c + 1) * RW_CHUNK, h * HEAD_DIM:(h + 1) * HEAD_DIM] = o_c[h]
    s_ref[0] = s

    yn = o_ref[...] * lng_ref[...] + lnb_ref[...]
    y_ref[...] = ((yn + post_ref[0]) * post_ref[1]).astype(BF16)


def _split_dot_left(m, x):
    hi = x.astype(BF16)
    lo = (x - hi.astype(F32)).astype(BF16)
    return _dot(m, hi) + _dot(m, lo)


def _rwkv_prompt(pb, params, *, batch, seq, tt):
    m = pb.shape[0]
    nt = seq // tt
    in_specs = [pl.BlockSpec((tt, B_COLS), lambda b, t: (b * nt + t, 0))] + [_const_spec(p.shape) for p in params]
    n_tiles = B_HEADS * tt // RW_CHUNK
    head_scratch = lambda dt: pltpu.VMEM((n_tiles, RW_CHUNK, HEAD_DIM), dt)
    return pl.pallas_call(
        _rwkv_prompt_kernel,
        grid=(batch, nt),
        in_specs=in_specs,
        out_specs=[pl.BlockSpec((tt, B_WIDTH), lambda b, t: (b * nt + t, 0)),
                   pl.BlockSpec((1, B_HEADS, HEAD_DIM, HEAD_DIM), lambda b, t: (b, 0, 0, 0))],
        out_shape=[jax.ShapeDtypeStruct((m, B_WIDTH), BF16),
                   jax.ShapeDtypeStruct((batch, B_HEADS, HEAD_DIM, HEAD_DIM), F32)],
        scratch_shapes=[pltpu.VMEM((8, B_COLS), F32), head_scratch(BF16), head_scratch(F32), head_scratch(BF16),
                        head_scratch(BF16), head_scratch(BF16), pltpu.VMEM((n_tiles, 1, HEAD_DIM), F32),
                        pltpu.VMEM((tt, B_WIDTH), F32), pltpu.VMEM((2, tt, B_WIDTH), F32)],
        compiler_params=pltpu.CompilerParams(dimension_semantics=("arbitrary", "arbitrary"),
                                             vmem_limit_bytes=VMEM_LIMIT),
        name="rwkv_prompt",
    )(pb, *params)


def _rwkv_step_kernel(first_layer, pb_ref, sh_ref, s_ref, mu_ref, w0_ref, w2_ref, a0_ref, a2_ref, g2_ref, kk_ref,
                      ka_ref, rk_ref, lng_ref, lnb_ref, bd_ref, *rest):
    if first_layer:
        y_ref, so_ref, row_ref, col_ref, yt_ref = rest
    else:
        _, y_ref, so_ref, row_ref, col_ref, yt_ref = rest
    h = pl.program_id(0)
    bd = bd_ref[...]

    @pl.when(h == 0)
    def _():
        r, k2, v, kk, a, log_decay, g = _rwkv_prep(pb_ref[...], sh_ref[...], mu_ref[...], w0_ref[...], w2_ref[...],
                                                   a0_ref[...], a2_ref[...], g2_ref[...], kk_ref[...], ka_ref[...], bd)
        for n, x in enumerate((r, k2, v, g)):
            row_ref[n] = x
        for n, x in enumerate((r, jnp.exp(log_decay), k2, v, kk, kk * a)):
            col_ref[n] = x.T

    rows = pl.ds(pl.multiple_of(h * HEAD_DIM, HEAD_DIM), HEAD_DIM)
    r_c, w_c, k_c, v_c, kk_c, b_c = (col_ref[n, rows, :] for n in range(6))
    for i in range(HEAD_DIM):
        s = s_ref[0, 0, i]
        s_kk = jnp.sum(s * kk_c, axis=0, keepdims=True)
        s_new = s * w_c - s_kk * b_c + v_c[i:i + 1] * k_c
        so_ref[0, 0, i] = s_new
        yt_ref[pl.ds(h * HEAD_DIM + i, 1), :] = jnp.sum(s_new * r_c, axis=0, keepdims=True)

    @pl.when(h == pl.num_programs(0) - 1)
    def _():
        y_ref[...] = _rwkv_post(yt_ref[...].T, row_ref[0], row_ref[1], row_ref[2], row_ref[3], rk_ref[...],
                                lng_ref[...], lnb_ref[...], bd).astype(BF16)


def _rwkv_step(pb, shift, state_t, states_out, params, layer):
    m = pb.shape[0]
    st_spec = pl.BlockSpec((1, 1, HEAD_DIM, HEAD_DIM, m), lambda h: (layer, h, 0, 0, 0))
    full = lambda width: pl.BlockSpec((m, width), lambda h: (0, 0))
    in_specs = [full(B_COLS), pl.BlockSpec((None, m, B_COLS), lambda h: (layer, 0, 0)), st_spec]
    in_specs += [_const_spec(p.shape) for p in params]
    args = [pb, shift, state_t, *params]
    aliases = {}
    if states_out is not None:
        in_specs.append(pl.BlockSpec(memory_space=pl.ANY))
        aliases = {len(args): 1}
        args.append(states_out)
    return pl.pallas_call(
        functools.partial(_rwkv_step_kernel, states_out is None),
        grid=(B_HEADS,),
        in_specs=in_specs,
        out_specs=[full(B_WIDTH), st_spec],
        out_shape=[jax.ShapeDtypeStruct((m, B_WIDTH), BF16), jax.ShapeDtypeStruct(state_t.shape, F32)],
        input_output_aliases=aliases,
        scratch_shapes=[pltpu.VMEM((4, m, B_WIDTH), F32), pltpu.VMEM((6, B_WIDTH, m), F32),
                        pltpu.VMEM((B_WIDTH, m), F32)],
        compiler_params=pltpu.CompilerParams(dimension_semantics=("arbitrary",), vmem_limit_bytes=VMEM_LIMIT),
        name="rwkv_sample",
    )(*args)


def _sb_chains(z_chains, mask_chains, carries, tri):
    units = [(c, n) for c in range(len(z_chains)) for n in range(len(z_chains[c]))]
    log_beta, log_keep, after, att = {}, {}, {}, {}
    for c, n in units:
        z = z_chains[c][n]
        log_beta[c, n] = jnp.minimum(z, 0.0) - jnp.log(1.0 + jnp.exp(-jnp.abs(z)))
        keep = log_beta[c, n] - z
        log_keep[c, n] = keep if mask_chains[c][n] is None else jnp.where(mask_chains[c][n], keep, 0.0)
    for u in units:
        after[u] = _dot(log_keep[u].astype(BF16), tri)
    carries = list(carries)
    keys = tri.shape[0]
    for c, n in units:
        w = jnp.exp(log_beta[c, n] + after[c, n][:, :keys] + carries[c])
        att[c, n] = (w if mask_chains[c][n] is None else jnp.where(mask_chains[c][n], w, 0.0)).astype(BF16)
        carries[c] = carries[c] + after[c, n][:, keys:]
    return [[att[c, n] for n in range(len(z_chains[c]))] for c in range(len(z_chains))], carries


def _suffix_and_total(n):
    later = lax.broadcasted_iota(jnp.int32, (n, 2 * n), 0) > lax.broadcasted_iota(jnp.int32, (n, 2 * n), 1)
    whole = lax.broadcasted_iota(jnp.int32, (n, 2 * n), 1) >= n
    return jnp.where(later | whole, 1.0, 0.0).astype(BF16)


def _sb_prompt_kernel(bias_ref, q_ref, k_ref, v_ref, o_ref, acc_ref, carry_ref):
    blk = q_ref.shape[1]
    i = pl.program_id(1)
    tri = _suffix_and_total(blk)
    causal = lax.broadcasted_iota(jnp.int32, (blk, blk), 1) < lax.broadcasted_iota(jnp.int32, (blk, blk), 0)
    acc_ref[...] = jnp.zeros_like(acc_ref)
    carry_ref[...] = jnp.zeros_like(carry_ref)

    def process(blocks, masks):
        offs = [pl.multiple_of(j * blk, blk) for j in blocks]
        zs = [[_dot_nt(q_ref[h], k_ref[h, pl.ds(off, blk), :]) + bias_ref[h] for off in offs] for h in range(C_HEADS)]
        att, carry = _sb_chains(zs, [masks] * C_HEADS, [carry_ref[h] for h in range(C_HEADS)], tri)
        pv = [[_dot(att[h][n], v_ref[h, pl.ds(off, blk), :]) for n, off in enumerate(offs)] for h in range(C_HEADS)]
        for h in range(C_HEADS):
            acc_ref[h] += functools.reduce(lambda a, b: a + b, pv[h])
            carry_ref[h] = carry[h]

    for rem in range(SB_GROUP):
        @pl.when(i % SB_GROUP == rem)
        def _():
            process([i - n for n in range(rem + 1)], [causal] + [None] * rem)

    def group(p, _):
        first = i - i % SB_GROUP - 1 - SB_GROUP * p
        process([first - n for n in range(SB_GROUP)], [None] * SB_GROUP)
        return 0

    lax.fori_loop(0, i // SB_GROUP, group, 0)

    for h in range(C_HEADS):
        o_ref[:, h * HEAD_DIM:(h + 1) * HEAD_DIM] = acc_ref[h].astype(BF16)


def _sb_prompt(bias, qh, kh, vh, *, batch, seq):
    m = qh.shape[1]
    nb = seq // SB_BLOCK
    kv_spec = pl.BlockSpec((C_HEADS, seq, HEAD_DIM), lambda b, i: (0, b, 0))
    return pl.pallas_call(
        _sb_prompt_kernel,
        grid=(batch, nb),
        in_specs=[pl.BlockSpec(memory_space=pltpu.SMEM),
                  pl.BlockSpec((C_HEADS, SB_BLOCK, HEAD_DIM), lambda b, i: (0, b * nb + i, 0)), kv_spec, kv_spec],
        out_specs=pl.BlockSpec((SB_BLOCK, C_WIDTH), lambda b, i: (b * nb + i, 0)),
        out_shape=jax.ShapeDtypeStruct((m, C_WIDTH), BF16),
        scratch_shapes=[pltpu.VMEM((C_HEADS, SB_BLOCK, HEAD_DIM), F32), pltpu.VMEM((C_HEADS, SB_BLOCK, SB_BLOCK), F32)],
        compiler_params=pltpu.CompilerParams(dimension_semantics=("arbitrary", "arbitrary"),
                                             vmem_limit_bytes=VMEM_LIMIT),
        name="sb_prompt",
    )(bias, qh, kh, vh)


def _sb_sample_kernel(n_seq, n_pages, page_size, pt_ref, bias_ref, q_ref, kn_ref, vn_ref, *refs):
    n_blocks = n_seq * n_pages
    k_refs = refs[:n_blocks]
    v_refs = refs[n_blocks:2 * n_blocks]
    o_ref = refs[2 * n_blocks]
    rows = 8
    tri = _suffix_and_total(page_size)
    r_col = lax.broadcasted_iota(jnp.int32, (rows, 1), 0)

    def by_row(per_head):
        out = per_head[C_HEADS - 1]
        for h in reversed(range(C_HEADS - 1)):
            out = jnp.where(r_col == h, per_head[h], out)
        return out

    bias = by_row([jnp.full((rows, 1), bias_ref[h], F32) for h in range(C_HEADS)])
    lanes = [slice(h * HEAD_DIM, (h + 1) * HEAD_DIM) for h in range(C_HEADS)]
    past = n_pages * page_size
    new_is_causal = (past + 0) < (past + 0)
    pages = list(reversed(range(n_pages)))

    z_chains, carries, w_new = [], [], []
    for s in range(n_seq):
        q_h = [jnp.broadcast_to(q_ref[s, :, lanes[h]], (rows, HEAD_DIM)) for h in range(C_HEADS)]
        q_b = [x.astype(BF16) for x in q_h]
        z_new = by_row([jnp.sum(q_h[h] * kn_ref[s, :, lanes[h]], axis=-1, keepdims=True)
                        for h in range(C_HEADS)]) + bias
        carries.append(jnp.broadcast_to(jnp.where(new_is_causal, -_softplus(z_new), 0.0), (rows, page_size)))
        w_new.append(jnp.where(new_is_causal, jnp.exp(-_softplus(-z_new)), 0.0))
        z_chains.append([by_row([_dot(q_b[h], k_refs[s * n_pages + p][0, 0, h].astype(BF16))
                                 for h in range(C_HEADS)]) + bias for p in pages])
    att, _ = _sb_chains(z_chains, [[None] * n_pages] * n_seq, carries, tri)
    for s in range(n_seq):
        for h in range(C_HEADS):
            pv = [_dot_nt(att[s][n], v_refs[s * n_pages + p][0, 0, h].astype(BF16)) for n, p in enumerate(pages)]
            out = w_new[s] * vn_ref[s, :, lanes[h]] + functools.reduce(lambda a, b: a + b, pv)
            o_ref[s, :, lanes[h]] = out[h:h + 1].astype(BF16)


def _sb_sample(page_table, bias, q, k_new, v_new, cache_k, cache_v, layer, *, n_seq):
    m = q.shape[0]
    n_pages = page_table.shape[1]
    page_size = cache_k.shape[2]
    cache_k = jnp.transpose(cache_k, (0, 1, 3, 4, 2))
    cache_v = jnp.transpose(cache_v, (0, 1, 3, 4, 2))
    pt = page_table.reshape(-1)
    row3 = lambda x: x.reshape(m, 1, C_WIDTH)
    row_spec = pl.BlockSpec((n_seq, 1, C_WIDTH), lambda b, pt: (b, 0, 0))

    def page_spec(s, p):
        return pl.BlockSpec((1, 1, C_HEADS, HEAD_DIM, page_size),
                            lambda b, pt: (layer, pt[(b * n_seq + s) * n_pages + p], 0, 0, 0))

    page_specs = [page_spec(s, p) for s in range(n_seq) for p in range(n_pages)]
    grid_spec = pltpu.PrefetchScalarGridSpec(
        num_scalar_prefetch=1,
        grid=(m // n_seq,),
        in_specs=[pl.BlockSpec(memory_space=pltpu.SMEM), row_spec, row_spec, row_spec] + page_specs + page_specs,
        out_specs=row_spec,
    )
    n_blocks = n_seq * n_pages
    out = pl.pallas_call(
        functools.partial(_sb_sample_kernel, n_seq, n_pages, page_size),
        grid_spec=grid_spec,
        out_shape=jax.ShapeDtypeStruct((m, 1, C_WIDTH), BF16),
        compiler_params=pltpu.CompilerParams(dimension_semantics=("arbitrary",), vmem_limit_bytes=VMEM_LIMIT),
        name="sb_sample",
    )(pt, bias, row3(q), row3(k_new), row3(v_new), *([cache_k] * n_blocks), *([cache_v] * n_blocks))
    return out.reshape(m, C_WIDTH)


def _tail_kernel(h_ref, ya_ref, yb_ref, yc_ref, pe_ref, wo_ref, fg_ref, wg_ref, wu_ref, wd_ref, pg_ref, wpg_ref,
                 wple_ref, out_ref):
    h = (h_ref[...] + _dot(ya_ref[...], wo_ref[0:A_WIDTH, :]) + _dot(yb_ref[...], wo_ref[A_WIDTH:A_WIDTH + B_WIDTH, :])
         + _dot(yc_ref[...], wo_ref[A_WIDTH + B_WIDTH:, :]))
    hn = _rms(h, fg_ref[...]).astype(BF16)
    gate = _dot(hn, wg_ref[...])
    act = (gate * jax.nn.sigmoid(gate) * _dot(hn, wu_ref[...])).astype(BF16)
    h = h + _dot(act, wd_ref[...])
    gate = jax.nn.sigmoid(_dot(_rms(h, pg_ref[...]).astype(BF16), wpg_ref[...]))
    out_ref[...] = h + _dot(pe_ref[...].astype(BF16), wple_ref[...]) * gate


def _tail(h, ya, yb, yc, pe, wo, fg, wg, wu, wd, pg, wpg, wple, *, tm, layer):
    m, d = h.shape
    row = lambda width: pl.BlockSpec((tm, width), lambda i: (i, 0))
    weights = (wo, fg, wg, wu, wd, pg, wpg, wple)
    return pl.pallas_call(
        _tail_kernel,
        grid=(m // tm,),
        in_specs=[row(d), row(A_WIDTH), row(B_WIDTH), row(C_WIDTH),
                  pl.BlockSpec((None, tm, pe.shape[2]), lambda i: (layer, i, 0))]
        + [_const_spec(w.shape) for w in weights],
        out_specs=row(d),
        out_shape=jax.ShapeDtypeStruct((m, d), F32),
        compiler_params=pltpu.CompilerParams(dimension_semantics=("arbitrary",), vmem_limit_bytes=VMEM_LIMIT),
        name="tail",
    )(h, ya, yb, yc, pe, *weights)


def _pad_rows(w, top, total):
    return jnp.pad(w, ((top, total - top - w.shape[0]), (0, 0)))


def kernel(x_prompt, x_sample, cache_k, cache_v, state_wkv, state_shift, page_table, p_prompt, p_sample, mix_norm, w_in, a_ln_g, a_ln_b, a_ws, a_bs, b_mu, b_w0, b_w2, b_a0, b_a2, b_g2, b_kk, b_ka, b_rk, b_ln_g, b_ln_b, c_qn, c_kn, c_bias, w_out, ffn_norm, w_gate, w_up, w_down, ple_norm, w_ple_gate, w_ple):
    depth = w_in.shape[0]
    bp, seq, d = x_prompt.shape
    bs = x_sample.shape[0]
    hp = x_prompt.reshape(bp * seq, d)
    hs = x_sample.reshape(bs, d)
    hd = lax.broadcasted_iota(jnp.int32, (B_WIDTH, B_WIDTH), 0) // HEAD_DIM
    bd = (hd == hd.T).astype(BF16)
    row = lambda x: x.reshape(1, -1)

    outs = {n: [] for n in ("wkvp", "shp", "ks", "vs", "shs", "avs")}
    pe_p = p_prompt.reshape(depth, bp * seq, -1)
    pe_s = p_sample.reshape(depth, bs, -1)
    state_t = jnp.transpose(state_wkv, (0, 2, 3, 4, 1))
    kv_shape = (depth, bp, C_HEADS, HEAD_DIM, seq)
    kv_bufs = None
    states_out = None
    for i in range(depth):
        w_in_b = w_in[i].astype(BF16)
        proj_common = (row(mix_norm[i]), w_in_b, row(a_ln_g[i]), row(a_ln_b[i]))
        bias_lanes = jnp.repeat(a_bs[i].T, HEAD_DIM, axis=1)
        w00_lanes = jnp.repeat(a_ws[i][:, 0, 0], HEAD_DIM).reshape(1, A_WIDTH)
        rw_params = (row(b_mu[i]), row(b_w0[i]), _pad_rows(b_w2[i], 0, LORA_WA).astype(BF16), row(b_a0[i]),
                     _pad_rows(b_a2[i], LORA_WA - b_a2.shape[1], LORA_WA).astype(BF16), b_g2[i].astype(BF16),
                     row(b_kk[i]), row(b_ka[i]), row(b_rk[i]), row(b_ln_g[i]), row(b_ln_b[i]), bd)
        tail_w = (w_out[i].astype(BF16), row(ffn_norm[i]), w_gate[i].astype(BF16), w_up[i].astype(BF16),
                  w_down[i].astype(BF16), row(ple_norm[i]), w_ple_gate[i].astype(BF16), w_ple[i].astype(BF16))
        qk_norm = (row(jnp.tile(c_qn[i], C_HEADS)), row(jnp.tile(c_kn[i], C_HEADS)), bd[:C_WIDTH, :C_WIDTH])

        ya, pb, qh, kh, vh, *kv_bufs = _proj(hp, *proj_common, a_ws[i], bias_lanes, *qk_norm, sample=False, tm=512,
                                             layer=i, seq=seq, kv_shape=kv_shape, kv_bufs=kv_bufs)
        yb, s_fin = _rwkv_prompt(pb, rw_params, batch=bp, seq=seq, tt=256)
        yc = _sb_prompt(c_bias[i], qh, kh, vh, batch=bp, seq=seq)
        hp = _tail(hp, ya, yb, yc, pe_p, *tail_w, tm=512, layer=i)
        outs["wkvp"].append(s_fin)
        outs["shp"].append(pb.reshape(bp, seq, B_COLS)[:, -1])

        ya, pb, q_n, k_n, v_n, a_vn = _proj(hs, *proj_common, w00_lanes, bias_lanes[0:1], *qk_norm, sample=True, tm=bs)
        yb, states_out = _rwkv_step(pb, state_shift, state_t, states_out, rw_params, i)
        yc = _sb_sample(page_table, c_bias[i], q_n, k_n, v_n, cache_k, cache_v, i, n_seq=4)
        hs = _tail(hs, ya, yb, yc, pe_s, *tail_w, tm=bs, layer=i)
        outs["ks"].append(k_n.reshape(bs, 1, C_HEADS, HEAD_DIM))
        outs["vs"].append(v_n.reshape(bs, 1, C_HEADS, HEAD_DIM))
        outs["shs"].append(pb)
        outs["avs"].append(a_vn.reshape(bs, 1, A_WIDTH))

    st = {n: jnp.stack(v) for n, v in outs.items()}
    k_prompt, v_prompt = (jnp.transpose(x, (0, 1, 4, 2, 3)) for x in kv_bufs)
    wkv_sample = jnp.transpose(states_out, (0, 4, 1, 2, 3))
    return (hp.reshape(bp, seq, d), hs.reshape(bs, 1, d), k_prompt, v_prompt, st["wkvp"], st["shp"],
            st["ks"], st["vs"], wkv_sample, st["shs"], st["avs"])
```

```python
import functools

import jax
import jax.numpy as jnp
from jax import lax
from jax.experimental import pallas as pl
from jax.experimental.pallas import tpu as pltpu

F32 = jnp.float32
BF16 = jnp.bfloat16

HEAD_DIM = 64
A_WIDTH = 256
A_GROUPS = 4
CHUNK = 128
B_WIDTH = 512
B_HEADS = 8
B_COLS = 1792
LORA_WA = 128
C_WIDTH = 256
C_HEADS = 4
SB_BLOCK = 128
SB_GROUP = 8
SB_SCALE = HEAD_DIM ** -0.5
RMS_EPS = 1e-6
LN_EPS = 1e-5
GN_EPS = 64e-5
RW_CHUNK = 64

VMEM_LIMIT = 56 * 1024 * 1024


def _dot(a, b):
    return jnp.dot(a, b, preferred_element_type=F32)


def _dot_nt(a, b):
    return lax.dot_general(a, b, (((1,), (1,)), ((), ())), preferred_element_type=F32)


def _dot_tn(a, b):
    return lax.dot_general(a, b, (((0,), (0,)), ((), ())), preferred_element_type=F32)


def _rms(x, g):
    return x * lax.rsqrt(jnp.mean(x * x, axis=-1, keepdims=True) + RMS_EPS) * g


def _softplus(x):
    return jnp.maximum(x, 0.0) + jnp.log1p(jnp.exp(-jnp.abs(x)))


def _const_spec(shape):
    nd = len(shape)
    return pl.BlockSpec(shape, lambda *_: (0,) * nd, pipeline_mode=pl.Buffered(1))


def _proj_kernel(sample, h_ref, g_ref, w_ref, lng_ref, lnb_ref, ws_ref, bs_ref, qn_ref, kn_ref, bd_ref, *rest):
    if sample:
        ya_ref, pb_ref, q_ref, k_ref, v_ref, av_ref = rest
    else:
        ya_ref, pb_ref, qh_ref, kh_ref, vh_ref, kt_ref, vt_ref = rest[-7:]
    tm = h_ref.shape[0]
    hn = _rms(h_ref[...], g_ref[...]).astype(BF16)

    pc = _dot(hn, w_ref[:, 2 * A_WIDTH + B_COLS:])
    q = pc[:, :C_WIDTH]
    k = pc[:, C_WIDTH:2 * C_WIDTH]
    v = pc[:, 2 * C_WIDTH:]
    bd = bd_ref[...]
    q = q * lax.rsqrt(_head_sum(q * q, bd) * (1.0 / HEAD_DIM) + RMS_EPS) * (qn_ref[...] * SB_SCALE)
    k = k * lax.rsqrt(_head_sum(k * k, bd) * (1.0 / HEAD_DIM) + RMS_EPS) * kn_ref[...]
    if sample:
        q_ref[...] = q
        k_ref[...] = k
        v_ref[...] = v
    else:
        for dst, x in ((qh_ref, q.astype(BF16)), (kh_ref, k.astype(BF16)), (vh_ref, v.astype(BF16))):
            for h in range(C_HEADS):
                dst[h] = x[:, h * HEAD_DIM:(h + 1) * HEAD_DIM]
        for dst, x in ((kt_ref, k.T), (vt_ref, v.T)):
            for h in range(C_HEADS):
                dst[0, 0, h] = x[h * HEAD_DIM:(h + 1) * HEAD_DIM, :]

    pa = _dot(hn, w_ref[:, 0:2 * A_WIDTH])
    a_u = pa[:, :A_WIDTH]
    a_v = pa[:, A_WIDTH:]
    xc = a_v - jnp.mean(a_v, axis=-1, keepdims=True)
    a_vn = xc * lax.rsqrt(jnp.mean(xc * xc, axis=-1, keepdims=True) + LN_EPS) * lng_ref[...] + lnb_ref[...]
    if sample:
        av_ref[...] = a_vn
        ya_ref[...] = (a_u * (ws_ref[...] * a_vn + bs_ref[...])).astype(BF16)
    else:
        causal = lax.broadcasted_iota(jnp.int32, (CHUNK, CHUNK), 0) >= lax.broadcasted_iota(jnp.int32, (CHUNK, CHUNK), 1)
        group = lax.broadcasted_iota(jnp.int32, (CHUNK, A_WIDTH), 1) // HEAD_DIM
        w_tril = [jnp.where(causal, ws_ref[g], 0.0).astype(BF16) for g in range(A_GROUPS)]
        for c in range(tm // CHUNK):
            rows = slice(c * CHUNK, (c + 1) * CHUNK)
            v_c = a_vn[rows]
            mixed = bs_ref[...]
            for g in range(A_GROUPS):
                mixed = mixed + _dot(w_tril[g], jnp.where(group == g, v_c, 0.0).astype(BF16))
            ya_ref[rows, :] = (a_u[rows] * mixed).astype(BF16)

    pb_ref[...] = _dot(hn, w_ref[:, 2 * A_WIDTH:2 * A_WIDTH + B_COLS])


def _proj(h, g, w, lng, lnb, ws, bs, qn, kn, bd, *, sample, tm, layer=0, seq=None, kv_shape=None, kv_bufs=None):
    m, d = h.shape
    row = lambda width: pl.BlockSpec((tm, width), lambda i: (i, 0))
    head = pl.BlockSpec((C_HEADS, tm, HEAD_DIM), lambda i: (0, i, 0))
    in_specs = [row(d), _const_spec(g.shape), _const_spec(w.shape), _const_spec(lng.shape), _const_spec(lnb.shape),
                _const_spec(ws.shape), _const_spec(bs.shape), _const_spec(qn.shape), _const_spec(kn.shape),
                _const_spec(bd.shape)]
    args = [h, g, w, lng, lnb, ws, bs, qn, kn, bd]
    aliases = {}
    if sample:
        out_shape = [jax.ShapeDtypeStruct((m, A_WIDTH), BF16), jax.ShapeDtypeStruct((m, B_COLS), F32),
                     jax.ShapeDtypeStruct((m, C_WIDTH), F32), jax.ShapeDtypeStruct((m, C_WIDTH), F32),
                     jax.ShapeDtypeStruct((m, C_WIDTH), F32), jax.ShapeDtypeStruct((m, A_WIDTH), F32)]
        out_specs = [row(A_WIDTH), row(B_COLS), row(C_WIDTH), row(C_WIDTH), row(C_WIDTH), row(A_WIDTH)]
    else:
        tiles = seq // tm
        hshape = jax.ShapeDtypeStruct((C_HEADS, m, HEAD_DIM), BF16)
        kv = jax.ShapeDtypeStruct(kv_shape, F32)
        kv_spec = pl.BlockSpec((1, 1, C_HEADS, HEAD_DIM, tm), lambda i: (layer, i // tiles, 0, 0, i % tiles))
        out_shape = [jax.ShapeDtypeStruct((m, A_WIDTH), BF16), jax.ShapeDtypeStruct((m, B_COLS), F32),
                     hshape, hshape, hshape, kv, kv]
        out_specs = [row(A_WIDTH), row(B_COLS), head, head, head, kv_spec, kv_spec]
        if kv_bufs is not None:
            in_specs += [pl.BlockSpec(memory_space=pl.ANY)] * 2
            aliases = {len(args): 5, len(args) + 1: 6}
            args += list(kv_bufs)
    return pl.pallas_call(
        functools.partial(_proj_kernel, sample),
        grid=(m // tm,),
        in_specs=in_specs,
        out_specs=out_specs,
        out_shape=out_shape,
        input_output_aliases=aliases,
        compiler_params=pltpu.CompilerParams(dimension_semantics=("arbitrary",), vmem_limit_bytes=VMEM_LIMIT),
        name="proj_sample" if sample else "proj_prompt",
    )(*args)


def _head_sum(x, bd):
    return _dot(x.astype(BF16), bd)


def _rwkv_prep(x, prev, mu, w0, w2p, a0, a2p, g2, k_k, k_a, bd):
    xs = x + (prev - x) * mu
    r = xs[:, 0:B_WIDTH]
    k = xs[:, B_WIDTH:2 * B_WIDTH]
    v = xs[:, 2 * B_WIDTH:3 * B_WIDTH]
    wa = xs[:, 3 * B_WIDTH:3 * B_WIDTH + LORA_WA]
    gl = xs[:, 3 * B_WIDTH + LORA_WA:]
    w_log = -_softplus(-(w0 + _dot(jnp.tanh(wa).astype(BF16), w2p))) - 0.5
    log_decay = -jnp.exp(w_log)
    a = jax.nn.sigmoid(a0 + _dot(wa.astype(BF16), a2p))
    g = _dot(jax.nn.sigmoid(gl).astype(BF16), g2)
    kk = k * k_k
    kk = kk * lax.rsqrt(_head_sum(kk * kk, bd) + 1e-12)
    k2 = k * (1.0 + (a - 1.0) * k_a)
    return r, k2, v, kk, a, log_decay, g


def _rwkv_post(y, r, k2, v, g, r_k, ln_g, ln_b, bd):
    yc = y - _head_sum(y, bd) * (1.0 / HEAD_DIM)
    yn = yc * lax.rsqrt(_head_sum(yc * yc, bd) * (1.0 / HEAD_DIM) + GN_EPS) * ln_g + ln_b
    bonus = _head_sum(r * k2 * r_k, bd) * v
    return (yn + bonus) * g


def _bmm(a, b):
    return lax.dot_general(a, b, (((2,), (1,)), ((0,), (0,))), preferred_element_type=F32)


def _bmm_nt(a, b):
    return lax.dot_general(a, b, (((2,), (2,)), ((0,), (0,))), preferred_element_type=F32)


def _bmm_tn(a, b):
    return lax.dot_general(a, b, (((1,), (1,)), ((0,), (0,))), preferred_element_type=F32)


def _rwkv_prompt_kernel(pb_ref, mu_ref, w0_ref, w2_ref, a0_ref, a2_ref, g2_ref, kk_ref, ka_ref, rk_ref, lng_ref,
                        lnb_ref, bd_ref, y_ref, s_ref,
                        prev_ref, at_ref, rt_ref, bt_ref, kt_ref, vv_ref, pc_ref, o_ref, post_ref):
    tt = pb_ref.shape[0]
    n_chunks = tt // RW_CHUNK
    t_idx = pl.program_id(1)

    @pl.when(t_idx == 0)
    def _():
        prev_ref[...] = jnp.zeros_like(prev_ref)
        s_ref[...] = jnp.zeros_like(s_ref)

    x = pb_ref[...]
    row_id = lax.broadcasted_iota(jnp.int32, x.shape, 0)
    prev = jnp.where(row_id == 0, prev_ref[0:1, :], pltpu.roll(x, 1, axis=0))
    prev_ref[0:1, :] = x[tt - 1:tt, :]
    bd = bd_ref[...]
    r, k2, v, kk, a, log_decay, g = _rwkv_prep(x, prev, mu_ref[...], w0_ref[...], w2_ref[...], a0_ref[...],
                                               a2_ref[...], g2_ref[...], kk_ref[...], ka_ref[...], bd)

    ti = lax.broadcasted_iota(jnp.int32, (tt, tt), 0)
    si = lax.broadcasted_iota(jnp.int32, (tt, tt), 1)
    same_chunk_lower = ((ti // RW_CHUNK) == (si // RW_CHUNK)) & (si <= ti)
    cum = _split_dot_left(jnp.where(same_chunk_lower, 1.0, 0.0).astype(BF16), log_decay)
    p_in = jnp.exp(cum)
    p_ex = jnp.exp(cum - log_decay)
    p_inv = jnp.exp(-cum)
    a_t = -kk * p_ex
    b_t = kk * a * p_inv
    k_t = k2 * p_inv
    r_t = r * p_in
    for h in range(B_HEADS):
        lanes = slice(h * HEAD_DIM, (h + 1) * HEAD_DIM)
        cols = [z[:, lanes] for z in (a_t, r_t, b_t, k_t, v, p_in)]
        for c in range(n_chunks):
            rows = slice(c * RW_CHUNK, (c + 1) * RW_CHUNK)
            n = c * B_HEADS + h
            at_ref[n] = cols[0][rows].astype(BF16)
            rt_ref[n] = cols[1][rows]
            bt_ref[n] = cols[2][rows].astype(BF16)
            kt_ref[n] = cols[3][rows].astype(BF16)
            vv_ref[n] = cols[4][rows].astype(BF16)
            pc_ref[n] = cols[5][(c + 1) * RW_CHUNK - 1:(c + 1) * RW_CHUNK]
    post_ref[0] = _head_sum(r * k2 * rk_ref[...], bd) * v
    post_ref[1] = g

    ci = lax.broadcasted_iota(jnp.int32, (RW_CHUNK, RW_CHUNK), 0)
    cj = lax.broadcasted_iota(jnp.int32, (RW_CHUNK, RW_CHUNK), 1)
    ci2 = lax.broadcasted_iota(jnp.int32, (RW_CHUNK, 2 * RW_CHUNK), 0)
    cj2 = lax.broadcasted_iota(jnp.int32, (RW_CHUNK, 2 * RW_CHUNK), 1) % RW_CHUNK
    strict = (cj2 < ci2)[None]
    incl = (cj2 <= ci2)[None]
    eye = jnp.where(ci == cj, 1.0, 0.0)[None]

    at = at_ref[...]
    rt = rt_ref[...]
    bt = bt_ref[...]
    kt = kt_ref[...]
    vv = vv_ref[...]
    bk = jnp.concatenate([bt, kt], axis=1)
    aa = _bmm_nt(jnp.concatenate([at, rt.astype(BF16)], axis=1), bk)
    a_a = jnp.where(strict, aa[:, :RW_CHUNK], 0.0)
    a_r = jnp.where(incl, aa[:, RW_CHUNK:], 0.0).astype(BF16)
    a_ab = a_a[:, :, :RW_CHUNK]
    a_ak = a_a[:, :, RW_CHUNK:].astype(BF16)
    a_rb = a_r[:, :, :RW_CHUNK]
    t_inv = eye + a_ab
    n_pow = a_ab
    for _ in range(5):
        n_b = n_pow.astype(BF16)
        n_pow = _bmm(n_b, n_b)
        t_inv = t_inv + _bmm(n_pow.astype(BF16), t_inv.astype(BF16))
    t_b = t_inv.astype(BF16)
    w_m = _bmm(t_b, at).astype(BF16)
    u0 = _bmm(t_b, _bmm(a_ak, vv).astype(BF16)).astype(BF16)
    uv = jnp.concatenate([u0, vv], axis=1)
    m_p = _bmm_tn(bt, w_m).astype(BF16)
    g_t = _bmm_tn(uv, bk)
    r_p = (rt + _bmm(a_rb, w_m)).astype(BF16)
    o_loc = _bmm(a_r, uv)

    s = s_ref[0]
    for c in range(n_chunks):
        tiles = slice(c * B_HEADS, (c + 1) * B_HEADS)
        sb = s.astype(BF16)
        o_c = _bmm_nt(r_p[tiles], sb) + o_loc[tiles]
        s = (s + _bmm_nt(sb, m_p[tiles]) + g_t[tiles]) * pc_ref[tiles]
        o_c = o_c - jnp.mean(o_c, axis=-1, keepdims=True)
        o_c = o_c * lax.rsqrt(jnp.mean(o_c * o_c, axis=-1, keepdims=True) + GN_EPS)
        for h in range(B_HEADS):
            o_ref[c * RW_CHUNK:(c + 1) * RW_CHUNK, h * HEAD_DIM:(h + 1) * HEAD_DIM] = o_c[h]
    s_ref[0] = s

    yn = o_ref[...] * lng_ref[...] + lnb_ref[...]
    y_ref[...] = ((yn + post_ref[0]) * post_ref[1]).astype(BF16)


def _split_dot_left(m, x):
    hi = x.astype(BF16)
    lo = (x - hi.astype(F32)).astype(BF16)
    return _dot(m, hi) + _dot(m, lo)


def _rwkv_prompt(pb, params, *, batch, seq, tt):
    m = pb.shape[0]
    nt = seq // tt
    in_specs = [pl.BlockSpec((tt, B_COLS), lambda b, t: (b * nt + t, 0))] + [_const_spec(p.shape) for p in params]
    n_tiles = B_HEADS * tt // RW_CHUNK
    head_scratch = lambda dt: pltpu.VMEM((n_tiles, RW_CHUNK, HEAD_DIM), dt)
    return pl.pallas_call(
        _rwkv_prompt_kernel,
        grid=(batch, nt),
        in_specs=in_specs,
        out_specs=[pl.BlockSpec((tt, B_WIDTH), lambda b, t: (b * nt + t, 0)),
                   pl.BlockSpec((1, B_HEADS, HEAD_DIM, HEAD_DIM), lambda b, t: (b, 0, 0, 0))],
        out_shape=[jax.ShapeDtypeStruct((m, B_WIDTH), BF16),
                   jax.ShapeDtypeStruct((batch, B_HEADS, HEAD_DIM, HEAD_DIM), F32)],
        scratch_shapes=[pltpu.VMEM((8, B_COLS), F32), head_scratch(BF16), head_scratch(F32), head_scratch(BF16),
                        head_scratch(BF16), head_scratch(BF16), pltpu.VMEM((n_tiles, 1, HEAD_DIM), F32),
                        pltpu.VMEM((tt, B_WIDTH), F32), pltpu.VMEM((2, tt, B_WIDTH), F32)],
        compiler_params=pltpu.CompilerParams(dimension_semantics=("arbitrary", "arbitrary"),
                                             vmem_limit_bytes=VMEM_LIMIT),
        name="rwkv_prompt",
    )(pb, *params)


def _rwkv_step_kernel(first_layer, pb_ref, sh_ref, s_ref, mu_ref, w0_ref, w2_ref, a0_ref, a2_ref, g2_ref, kk_ref,
                      ka_ref, rk_ref, lng_ref, lnb_ref, bd_ref, *rest):
    if first_layer:
        y_ref, so_ref, row_ref, col_ref, yt_ref = rest
    else:
        _, y_ref, so_ref, row_ref, col_ref, yt_ref = rest
    h = pl.program_id(0)
    bd = bd_ref[...]

    @pl.when(h == 0)
    def _():
        r, k2, v, kk, a, log_decay, g = _rwkv_prep(pb_ref[...], sh_ref[...], mu_ref[...], w0_ref[...], w2_ref[...],
                                                   a0_ref[...], a2_ref[...], g2_ref[...], kk_ref[...], ka_ref[...], bd)
        for n, x in enumerate((r, k2, v, g)):
            row_ref[n] = x
        for n, x in enumerate((r, jnp.exp(log_decay), k2, v, kk, kk * a)):
            col_ref[n] = x.T

    rows = pl.ds(pl.multiple_of(h * HEAD_DIM, HEAD_DIM), HEAD_DIM)
    r_c, w_c, k_c, v_c, kk_c, b_c = (col_ref[n, rows, :] for n in range(6))
    for i in range(HEAD_DIM):
        s = s_ref[0, 0, i]
        s_kk = jnp.sum(s * kk_c, axis=0, keepdims=True)
        s_new = s * w_c - s_kk * b_c + v_c[i:i + 1] * k_c
        so_ref[0, 0, i] = s_new
        yt_ref[pl.ds(h * HEAD_DIM + i, 1), :] = jnp.sum(s_new * r_c, axis=0, keepdims=True)

    @pl.when(h == pl.num_programs(0) - 1)
    def _():
        y_ref[...] = _rwkv_post(yt_ref[...].T, row_ref[0], row_ref[1], row_ref[2], row_ref[3], rk_ref[...],
                                lng_ref[...], lnb_ref[...], bd).astype(BF16)


def _rwkv_step(pb, shift, state_t, states_out, params, layer):
    m = pb.shape[0]
    st_spec = pl.BlockSpec((1, 1, HEAD_DIM, HEAD_DIM, m), lambda h: (layer, h, 0, 0, 0))
    full = lambda width: pl.BlockSpec((m, width), lambda h: (0, 0))
    in_specs = [full(B_COLS), pl.BlockSpec((None, m, B_COLS), lambda h: (layer, 0, 0)), st_spec]
    in_specs += [_const_spec(p.shape) for p in params]
    args = [pb, shift, state_t, *params]
    aliases = {}
    if states_out is not None:
        in_specs.append(pl.BlockSpec(memory_space=pl.ANY))
        aliases = {len(args): 1}
        args.append(states_out)
    return pl.pallas_call(
        functools.partial(_rwkv_step_kernel, states_out is None),
        grid=(B_HEADS,),
        in_specs=in_specs,
        out_specs=[full(B_WIDTH), st_spec],
        out_shape=[jax.ShapeDtypeStruct((m, B_WIDTH), BF16), jax.ShapeDtypeStruct(state_t.shape, F32)],
        input_output_aliases=aliases,
        scratch_shapes=[pltpu.VMEM((4, m, B_WIDTH), F32), pltpu.VMEM((6, B_WIDTH, m), F32),
                        pltpu.VMEM((B_WIDTH, m), F32)],
        compiler_params=pltpu.CompilerParams(dimension_semantics=("arbitrary",), vmem_limit_bytes=VMEM_LIMIT),
        name="rwkv_sample",
    )(*args)


def _sb_chains(z_chains, mask_chains, carries, tri):
    units = [(c, n) for c in range(len(z_chains)) for n in range(len(z_chains[c]))]
    log_beta, log_keep, after, att = {}, {}, {}, {}
    for c, n in units:
        z = z_chains[c][n]
        log_beta[c, n] = jnp.minimum(z, 0.0) - jnp.log(1.0 + jnp.exp(-jnp.abs(z)))
        keep = log_beta[c, n] - z
        log_keep[c, n] = keep if mask_chains[c][n] is None else jnp.where(mask_chains[c][n], keep, 0.0)
    for u in units:
        after[u] = _dot(log_keep[u].astype(BF16), tri)
    carries = list(carries)
    keys = tri.shape[0]
    for c, n in units:
        w = jnp.exp(log_beta[c, n] + after[c, n][:, :keys] + carries[c])
        att[c, n] = (w if mask_chains[c][n] is None else jnp.where(mask_chains[c][n], w, 0.0)).astype(BF16)
        carries[c] = carries[c] + after[c, n][:, keys:]
    return [[att[c, n] for n in range(len(z_chains[c]))] for c in range(len(z_chains))], carries


def _suffix_and_total(n):
    later = lax.broadcasted_iota(jnp.int32, (n, 2 * n), 0) > lax.broadcasted_iota(jnp.int32, (n, 2 * n), 1)
    whole = lax.broadcasted_iota(jnp.int32, (n, 2 * n), 1) >= n
    return jnp.where(later | whole, 1.0, 0.0).astype(BF16)


def _sb_prompt_kernel(bias_ref, q_ref, k_ref, v_ref, o_ref, acc_ref, carry_ref):
    blk = q_ref.shape[1]
    i = pl.program_id(1)
    tri = _suffix_and_total(blk)
    causal = lax.broadcasted_iota(jnp.int32, (blk, blk), 1) < lax.broadcasted_iota(jnp.int32, (blk, blk), 0)
    acc_ref[...] = jnp.zeros_like(acc_ref)
    carry_ref[...] = jnp.zeros_like(carry_ref)

    def process(blocks, masks):
        offs = [pl.multiple_of(j * blk, blk) for j in blocks]
        zs = [[_dot_nt(q_ref[h], k_ref[h, pl.ds(off, blk), :]) + bias_ref[h] for off in offs] for h in range(C_HEADS)]
        att, carry = _sb_chains(zs, [masks] * C_HEADS, [carry_ref[h] for h in range(C_HEADS)], tri)
        pv = [[_dot(att[h][n], v_ref[h, pl.ds(off, blk), :]) for n, off in enumerate(offs)] for h in range(C_HEADS)]
        for h in range(C_HEADS):
            acc_ref[h] += functools.reduce(lambda a, b: a + b, pv[h])
            carry_ref[h] = carry[h]

    for rem in range(SB_GROUP):
        @pl.when(i % SB_GROUP == rem)
        def _():
            process([i - n for n in range(rem + 1)], [causal] + [None] * rem)

    def group(p, _):
        first = i - i % SB_GROUP - 1 - SB_GROUP * p
        process([first - n for n in range(SB_GROUP)], [None] * SB_GROUP)
        return 0

    lax.fori_loop(0, i // SB_GROUP, group, 0)

    for h in range(C_HEADS):
        o_ref[:, h * HEAD_DIM:(h + 1) * HEAD_DIM] = acc_ref[h].astype(BF16)


def _sb_prompt(bias, qh, kh, vh, *, batch, seq):
    m = qh.shape[1]
    nb = seq // SB_BLOCK
    kv_spec = pl.BlockSpec((C_HEADS, seq, HEAD_DIM), lambda b, i: (0, b, 0))
    return pl.pallas_call(
        _sb_prompt_kernel,
        grid=(batch, nb),
        in_specs=[pl.BlockSpec(memory_space=pltpu.SMEM),
                  pl.BlockSpec((C_HEADS, SB_BLOCK, HEAD_DIM), lambda b, i: (0, b * nb + i, 0)), kv_spec, kv_spec],
        out_specs=pl.BlockSpec((SB_BLOCK, C_WIDTH), lambda b, i: (b * nb + i, 0)),
        out_shape=jax.ShapeDtypeStruct((m, C_WIDTH), BF16),
        scratch_shapes=[pltpu.VMEM((C_HEADS, SB_BLOCK, HEAD_DIM), F32), pltpu.VMEM((C_HEADS, SB_BLOCK, SB_BLOCK), F32)],
        compiler_params=pltpu.CompilerParams(dimension_semantics=("arbitrary", "arbitrary"),
                                             vmem_limit_bytes=VMEM_LIMIT),
        name="sb_prompt",
    )(bias, qh, kh, vh)


def _sb_sample_kernel(n_seq, n_pages, page_size, pt_ref, bias_ref, q_ref, kn_ref, vn_ref, *refs):
    n_blocks = n_seq * n_pages
    k_refs = refs[:n_blocks]
    v_refs = refs[n_blocks:2 * n_blocks]
    o_ref = refs[2 * n_blocks]
    rows = 8
    tri = _suffix_and_total(page_size)
    r_col = lax.broadcasted_iota(jnp.int32, (rows, 1), 0)
    bias = jnp.zeros((rows, 1), F32)
    for h in range(C_HEADS):
        bias = jnp.where(r_col == h, bias_ref[h], bias)
    head_lanes = (lax.broadcasted_iota(jnp.int32, (rows, C_WIDTH), 1) // HEAD_DIM
                  == lax.broadcasted_iota(jnp.int32, (rows, C_WIDTH), 0))
    past = n_pages * page_size
    new_is_causal = (past + 0) < (past + 0)
    pages = list(reversed(range(n_pages)))

    z_chains, carries, w_new = [], [], []
    for s in range(n_seq):
        q = jnp.where(head_lanes, q_ref[s], 0.0)
        q_b = q.astype(BF16)
        z_new = jnp.sum(q * kn_ref[s], axis=-1, keepdims=True) + bias
        carries.append(jnp.broadcast_to(jnp.where(new_is_causal, -_softplus(z_new), 0.0), (rows, page_size)))
        w_new.append(jnp.where(new_is_causal, jnp.exp(-_softplus(-z_new)), 0.0))
        z_chains.append([_dot(q_b, k_refs[s * n_pages + p][0, 0].astype(BF16)) + bias for p in pages])
    att, _ = _sb_chains(z_chains, [[None] * n_pages] * n_seq, carries, tri)
    for s in range(n_seq):
        pv = [_dot_nt(att[s][n], v_refs[s * n_pages + p][0, 0].astype(BF16)) for n, p in enumerate(pages)]
        out = w_new[s] * vn_ref[s] + functools.reduce(lambda a, b: a + b, pv)
        o_ref[s] = jnp.sum(jnp.where(head_lanes, out, 0.0), axis=0, keepdims=True).astype(BF16)


def _sb_sample(page_table, bias, q, k_new, v_new, cache_k, cache_v, layer, *, n_seq):
    m = q.shape[0]
    n_pages = page_table.shape[1]
    page_size = cache_k.shape[2]
    to_pages = lambda c: jnp.transpose(c, (0, 1, 3, 4, 2)).reshape(c.shape[0], c.shape[1], C_WIDTH, page_size)
    cache_k, cache_v = to_pages(cache_k), to_pages(cache_v)
    pt = page_table.reshape(-1)
    row3 = lambda x: x.reshape(m, 1, C_WIDTH)
    row_spec = pl.BlockSpec((n_seq, 1, C_WIDTH), lambda b, pt: (b, 0, 0))

    def page_spec(s, p):
        return pl.BlockSpec((1, 1, C_WIDTH, page_size),
                            lambda b, pt: (layer, pt[(b * n_seq + s) * n_pages + p], 0, 0))

    page_specs = [page_spec(s, p) for s in range(n_seq) for p in range(n_pages)]
    grid_spec = pltpu.PrefetchScalarGridSpec(
        num_scalar_prefetch=1,
        grid=(m // n_seq,),
        in_specs=[pl.BlockSpec(memory_space=pltpu.SMEM), row_spec, row_spec, row_spec] + page_specs + page_specs,
        out_specs=row_spec,
    )
    n_blocks = n_seq * n_pages
    out = pl.pallas_call(
        functools.partial(_sb_sample_kernel, n_seq, n_pages, page_size),
        grid_spec=grid_spec,
        out_shape=jax.ShapeDtypeStruct((m, 1, C_WIDTH), BF16),
        compiler_params=pltpu.CompilerParams(dimension_semantics=("arbitrary",), vmem_limit_bytes=VMEM_LIMIT),
        name="sb_sample",
    )(pt, bias, row3(q), row3(k_new), row3(v_new), *([cache_k] * n_blocks), *([cache_v] * n_blocks))
    return out.reshape(m, C_WIDTH)


def _tail_kernel(h_ref, ya_ref, yb_ref, yc_ref, pe_ref, wo_ref, fg_ref, wg_ref, wu_ref, wd_ref, pg_ref, wpg_ref,
                 wple_ref, out_ref):
    h = (h_ref[...] + _dot(ya_ref[...], wo_ref[0:A_WIDTH, :]) + _dot(yb_ref[...], wo_ref[A_WIDTH:A_WIDTH + B_WIDTH, :])
         + _dot(yc_ref[...], wo_ref[A_WIDTH + B_WIDTH:, :]))
    hn = _rms(h, fg_ref[...]).astype(BF16)
    gate = _dot(hn, wg_ref[...])
    act = (gate * jax.nn.sigmoid(gate) * _dot(hn, wu_ref[...])).astype(BF16)
    h = h + _dot(act, wd_ref[...])
    gate = jax.nn.sigmoid(_dot(_rms(h, pg_ref[...]).astype(BF16), wpg_ref[...]))
    out_ref[...] = h + _dot(pe_ref[...].astype(BF16), wple_ref[...]) * gate


def _tail(h, ya, yb, yc, pe, wo, fg, wg, wu, wd, pg, wpg, wple, *, tm, layer):
    m, d = h.shape
    row = lambda width: pl.BlockSpec((tm, width), lambda i: (i, 0))
    weights = (wo, fg, wg, wu, wd, pg, wpg, wple)
    return pl.pallas_call(
        _tail_kernel,
        grid=(m // tm,),
        in_specs=[row(d), row(A_WIDTH), row(B_WIDTH), row(C_WIDTH),
                  pl.BlockSpec((None, tm, pe.shape[2]), lambda i: (layer, i, 0))]
        + [_const_spec(w.shape) for w in weights],
        out_specs=row(d),
        out_shape=jax.ShapeDtypeStruct((m, d), F32),
        compiler_params=pltpu.CompilerParams(dimension_semantics=("arbitrary",), vmem_limit_bytes=VMEM_LIMIT),
        name="tail",
    )(h, ya, yb, yc, pe, *weights)


def _pad_rows(w, top, total):
    return jnp.pad(w, ((top, total - top - w.shape[0]), (0, 0)))


def kernel(x_prompt, x_sample, cache_k, cache_v, state_wkv, state_shift, page_table, p_prompt, p_sample, mix_norm, w_in, a_ln_g, a_ln_b, a_ws, a_bs, b_mu, b_w0, b_w2, b_a0, b_a2, b_g2, b_kk, b_ka, b_rk, b_ln_g, b_ln_b, c_qn, c_kn, c_bias, w_out, ffn_norm, w_gate, w_up, w_down, ple_norm, w_ple_gate, w_ple):
    depth = w_in.shape[0]
    bp, seq, d = x_prompt.shape
    bs = x_sample.shape[0]
    hp = x_prompt.reshape(bp * seq, d)
    hs = x_sample.reshape(bs, d)
    hd = lax.broadcasted_iota(jnp.int32, (B_WIDTH, B_WIDTH), 0) // HEAD_DIM
    bd = (hd == hd.T).astype(BF16)
    row = lambda x: x.reshape(1, -1)

    outs = {n: [] for n in ("wkvp", "shp", "ks", "vs", "shs", "avs")}
    pe_p = p_prompt.reshape(depth, bp * seq, -1)
    pe_s = p_sample.reshape(depth, bs, -1)
    state_t = jnp.transpose(state_wkv, (0, 2, 3, 4, 1))
    kv_shape = (depth, bp, C_HEADS, HEAD_DIM, seq)
    kv_bufs = None
    states_out = None
    for i in range(depth):
        w_in_b = w_in[i].astype(BF16)
        proj_common = (row(mix_norm[i]), w_in_b, row(a_ln_g[i]), row(a_ln_b[i]))
        bias_lanes = jnp.repeat(a_bs[i].T, HEAD_DIM, axis=1)
        w00_lanes = jnp.repeat(a_ws[i][:, 0, 0], HEAD_DIM).reshape(1, A_WIDTH)
        rw_params = (row(b_mu[i]), row(b_w0[i]), _pad_rows(b_w2[i], 0, LORA_WA).astype(BF16), row(b_a0[i]),
                     _pad_rows(b_a2[i], LORA_WA - b_a2.shape[1], LORA_WA).astype(BF16), b_g2[i].astype(BF16),
                     row(b_kk[i]), row(b_ka[i]), row(b_rk[i]), row(b_ln_g[i]), row(b_ln_b[i]), bd)
        tail_w = (w_out[i].astype(BF16), row(ffn_norm[i]), w_gate[i].astype(BF16), w_up[i].astype(BF16),
                  w_down[i].astype(BF16), row(ple_norm[i]), w_ple_gate[i].astype(BF16), w_ple[i].astype(BF16))
        qk_norm = (row(jnp.tile(c_qn[i], C_HEADS)), row(jnp.tile(c_kn[i], C_HEADS)), bd[:C_WIDTH, :C_WIDTH])

        ya, pb, qh, kh, vh, *kv_bufs = _proj(hp, *proj_common, a_ws[i], bias_lanes, *qk_norm, sample=False, tm=512,
                                             layer=i, seq=seq, kv_shape=kv_shape, kv_bufs=kv_bufs)
        yb, s_fin = _rwkv_prompt(pb, rw_params, batch=bp, seq=seq, tt=256)
        yc = _sb_prompt(c_bias[i], qh, kh, vh, batch=bp, seq=seq)
        hp = _tail(hp, ya, yb, yc, pe_p, *tail_w, tm=512, layer=i)
        outs["wkvp"].append(s_fin)
        outs["shp"].append(pb.reshape(bp, seq, B_COLS)[:, -1])

        ya, pb, q_n, k_n, v_n, a_vn = _proj(hs, *proj_common, w00_lanes, bias_lanes[0:1], *qk_norm, sample=True, tm=bs)
        yb, states_out = _rwkv_step(pb, state_shift, state_t, states_out, rw_params, i)
        yc = _sb_sample(page_table, c_bias[i], q_n, k_n, v_n, cache_k, cache_v, i, n_seq=4)
        hs = _tail(hs, ya, yb, yc, pe_s, *tail_w, tm=bs, layer=i)
        outs["ks"].append(k_n.reshape(bs, 1, C_HEADS, HEAD_DIM))
        outs["vs"].append(v_n.reshape(bs, 1, C_HEADS, HEAD_DIM))
        outs["shs"].append(pb)
        outs["avs"].append(a_vn.reshape(bs, 1, A_WIDTH))

    st = {n: jnp.stack(v) for n, v in outs.items()}
    k_prompt, v_prompt = (jnp.transpose(x, (0, 1, 4, 2, 3)) for x in kv_bufs)
    wkv_sample = jnp.transpose(states_out, (0, 4, 1, 2, 3))
    return (hp.reshape(bp, seq, d), hs.reshape(bs, 1, d), k_prompt, v_prompt, st["wkvp"], st["shp"],
            st["ks"], st["vs"], wkv_sample, st["shs"], st["avs"])
```

```python
import functools

import jax
import jax.numpy as jnp
from jax import lax
from jax.experimental import pallas as pl
from jax.experimental.pallas import tpu as pltpu

F32 = jnp.float32
BF16 = jnp.bfloat16

HEAD_DIM = 64
A_WIDTH = 256
A_GROUPS = 4
CHUNK = 128
B_WIDTH = 512
B_HEADS = 8
B_COLS = 1792
LORA_WA = 128
C_WIDTH = 256
C_HEADS = 4
SB_BLOCK = 128
SB_GROUP = 8
SB_SCALE = HEAD_DIM ** -0.5
RMS_EPS = 1e-6
LN_EPS = 1e-5
GN_EPS = 64e-5
RW_CHUNK = 64

VMEM_LIMIT = 56 * 1024 * 1024


def _dot(a, b):
    return jnp.dot(a, b, preferred_element_type=F32)


def _dot_nt(a, b):
    return lax.dot_general(a, b, (((1,), (1,)), ((), ())), preferred_element_type=F32)


def _dot_tn(a, b):
    return lax.dot_general(a, b, (((0,), (0,)), ((), ())), preferred_element_type=F32)


def _rms(x, g):
    return x * lax.rsqrt(jnp.mean(x * x, axis=-1, keepdims=True) + RMS_EPS) * g


def _softplus(x):
    return jnp.maximum(x, 0.0) + jnp.log(1.0 + jnp.exp(-jnp.abs(x)))


def _const_spec(shape):
    nd = len(shape)
    return pl.BlockSpec(shape, lambda *_: (0,) * nd, pipeline_mode=pl.Buffered(1))


def _proj_kernel(sample, h_ref, g_ref, w_ref, lng_ref, lnb_ref, ws_ref, bs_ref, qn_ref, kn_ref, bd_ref, *rest):
    if sample:
        ya_ref, pb_ref, q_ref, k_ref, v_ref, av_ref = rest
    else:
        ya_ref, pb_ref, qh_ref, kh_ref, vh_ref, kt_ref, vt_ref = rest[-7:]
    tm = h_ref.shape[0]
    hn = _rms(h_ref[...], g_ref[...]).astype(BF16)

    pc = _dot(hn, w_ref[:, 2 * A_WIDTH + B_COLS:])
    q = pc[:, :C_WIDTH]
    k = pc[:, C_WIDTH:2 * C_WIDTH]
    v = pc[:, 2 * C_WIDTH:]
    bd = bd_ref[...]
    q = q * lax.rsqrt(_head_sum(q * q, bd) * (1.0 / HEAD_DIM) + RMS_EPS) * (qn_ref[...] * SB_SCALE)
    k = k * lax.rsqrt(_head_sum(k * k, bd) * (1.0 / HEAD_DIM) + RMS_EPS) * kn_ref[...]
    if sample:
        q_ref[...] = q
        k_ref[...] = k
        v_ref[...] = v
    else:
        for dst, x in ((qh_ref, q.astype(BF16)), (kh_ref, k.astype(BF16)), (vh_ref, v.astype(BF16))):
            for h in range(C_HEADS):
                dst[h] = x[:, h * HEAD_DIM:(h + 1) * HEAD_DIM]
        for dst, x in ((kt_ref, k.T), (vt_ref, v.T)):
            for h in range(C_HEADS):
                dst[0, 0, h] = x[h * HEAD_DIM:(h + 1) * HEAD_DIM, :]

    pa = _dot(hn, w_ref[:, 0:2 * A_WIDTH])
    a_u = pa[:, :A_WIDTH]
    a_v = pa[:, A_WIDTH:]
    xc = a_v - jnp.mean(a_v, axis=-1, keepdims=True)
    a_vn = xc * lax.rsqrt(jnp.mean(xc * xc, axis=-1, keepdims=True) + LN_EPS) * lng_ref[...] + lnb_ref[...]
    if sample:
        av_ref[...] = a_vn
        ya_ref[...] = (a_u * (ws_ref[...] * a_vn + bs_ref[...])).astype(BF16)
    else:
        causal = lax.broadcasted_iota(jnp.int32, (CHUNK, CHUNK), 0) >= lax.broadcasted_iota(jnp.int32, (CHUNK, CHUNK), 1)
        group = lax.broadcasted_iota(jnp.int32, (CHUNK, A_WIDTH), 1) // HEAD_DIM
        w_tril = [jnp.where(causal, ws_ref[g], 0.0).astype(BF16) for g in range(A_GROUPS)]
        for c in range(tm // CHUNK):
            rows = slice(c * CHUNK, (c + 1) * CHUNK)
            v_c = a_vn[rows]
            mixed = bs_ref[...]
            for g in range(A_GROUPS):
                mixed = mixed + _dot(w_tril[g], jnp.where(group == g, v_c, 0.0).astype(BF16))
            ya_ref[rows, :] = (a_u[rows] * mixed).astype(BF16)

    pb_ref[...] = _dot(hn, w_ref[:, 2 * A_WIDTH:2 * A_WIDTH + B_COLS])


def _proj(h, g, w, lng, lnb, ws, bs, qn, kn, bd, *, sample, tm, layer=0, seq=None, kv_shape=None, kv_bufs=None):
    m, d = h.shape
    row = lambda width: pl.BlockSpec((tm, width), lambda i: (i, 0))
    head = pl.BlockSpec((C_HEADS, tm, HEAD_DIM), lambda i: (0, i, 0))
    in_specs = [row(d), _const_spec(g.shape), _const_spec(w.shape), _const_spec(lng.shape), _const_spec(lnb.shape),
                _const_spec(ws.shape), _const_spec(bs.shape), _const_spec(qn.shape), _const_spec(kn.shape),
                _const_spec(bd.shape)]
    args = [h, g, w, lng, lnb, ws, bs, qn, kn, bd]
    aliases = {}
    if sample:
        out_shape = [jax.ShapeDtypeStruct((m, A_WIDTH), BF16), jax.ShapeDtypeStruct((m, B_COLS), F32),
                     jax.ShapeDtypeStruct((m, C_WIDTH), F32), jax.ShapeDtypeStruct((m, C_WIDTH), F32),
                     jax.ShapeDtypeStruct((m, C_WIDTH), F32), jax.ShapeDtypeStruct((m, A_WIDTH), F32)]
        out_specs = [row(A_WIDTH), row(B_COLS), row(C_WIDTH), row(C_WIDTH), row(C_WIDTH), row(A_WIDTH)]
    else:
        tiles = seq // tm
        hshape = jax.ShapeDtypeStruct((C_HEADS, m, HEAD_DIM), BF16)
        kv = jax.ShapeDtypeStruct(kv_shape, F32)
        kv_spec = pl.BlockSpec((1, 1, C_HEADS, HEAD_DIM, tm), lambda i: (layer, i // tiles, 0, 0, i % tiles))
        out_shape = [jax.ShapeDtypeStruct((m, A_WIDTH), BF16), jax.ShapeDtypeStruct((m, B_COLS), F32),
                     hshape, hshape, hshape, kv, kv]
        out_specs = [row(A_WIDTH), row(B_COLS), head, head, head, kv_spec, kv_spec]
        if kv_bufs is not None:
            in_specs += [pl.BlockSpec(memory_space=pl.ANY)] * 2
            aliases = {len(args): 5, len(args) + 1: 6}
            args += list(kv_bufs)
    return pl.pallas_call(
        functools.partial(_proj_kernel, sample),
        grid=(m // tm,),
        in_specs=in_specs,
        out_specs=out_specs,
        out_shape=out_shape,
        input_output_aliases=aliases,
        compiler_params=pltpu.CompilerParams(dimension_semantics=("arbitrary",), vmem_limit_bytes=VMEM_LIMIT),
        name="proj_sample" if sample else "proj_prompt",
    )(*args)


def _head_sum(x, bd):
    return _dot(x.astype(BF16), bd)


def _rwkv_prep(x, prev, mu, w0, w2p, a0, a2p, g2, k_k, k_a, bd):
    xs = x + (prev - x) * mu
    r = xs[:, 0:B_WIDTH]
    k = xs[:, B_WIDTH:2 * B_WIDTH]
    v = xs[:, 2 * B_WIDTH:3 * B_WIDTH]
    wa = xs[:, 3 * B_WIDTH:3 * B_WIDTH + LORA_WA]
    gl = xs[:, 3 * B_WIDTH + LORA_WA:]
    w_log = -_softplus(-(w0 + _dot(jnp.tanh(wa).astype(BF16), w2p))) - 0.5
    log_decay = -jnp.exp(w_log)
    a = jax.nn.sigmoid(a0 + _dot(wa.astype(BF16), a2p))
    g = _dot(jax.nn.sigmoid(gl).astype(BF16), g2)
    kk = k * k_k
    kk = kk * lax.rsqrt(_head_sum(kk * kk, bd) + 1e-12)
    k2 = k * (1.0 + (a - 1.0) * k_a)
    return r, k2, v, kk, a, log_decay, g


def _rwkv_post(y, r, k2, v, g, r_k, ln_g, ln_b, bd):
    yc = y - _head_sum(y, bd) * (1.0 / HEAD_DIM)
    yn = yc * lax.rsqrt(_head_sum(yc * yc, bd) * (1.0 / HEAD_DIM) + GN_EPS) * ln_g + ln_b
    bonus = _head_sum(r * k2 * r_k, bd) * v
    return (yn + bonus) * g


def _bmm(a, b):
    return lax.dot_general(a, b, (((2,), (1,)), ((0,), (0,))), preferred_element_type=F32)


def _bmm_nt(a, b):
    return lax.dot_general(a, b, (((2,), (2,)), ((0,), (0,))), preferred_element_type=F32)


def _bmm_tn(a, b):
    return lax.dot_general(a, b, (((1,), (1,)), ((0,), (0,))), preferred_element_type=F32)


def _rwkv_prompt_kernel(pb_ref, mu_ref, w0_ref, w2_ref, a0_ref, a2_ref, g2_ref, kk_ref, ka_ref, rk_ref, lng_ref,
                        lnb_ref, bd_ref, y_ref, s_ref,
                        prev_ref, at_ref, rt_ref, bt_ref, kt_ref, vv_ref, pc_ref, o_ref, post_ref):
    tt = pb_ref.shape[0]
    n_chunks = tt // RW_CHUNK
    t_idx = pl.program_id(1)

    @pl.when(t_idx == 0)
    def _():
        prev_ref[...] = jnp.zeros_like(prev_ref)
        s_ref[...] = jnp.zeros_like(s_ref)

    x = pb_ref[...]
    row_id = lax.broadcasted_iota(jnp.int32, x.shape, 0)
    prev = jnp.where(row_id == 0, prev_ref[0:1, :], pltpu.roll(x, 1, axis=0))
    prev_ref[0:1, :] = x[tt - 1:tt, :]
    bd = bd_ref[...]
    r, k2, v, kk, a, log_decay, g = _rwkv_prep(x, prev, mu_ref[...], w0_ref[...], w2_ref[...], a0_ref[...],
                                               a2_ref[...], g2_ref[...], kk_ref[...], ka_ref[...], bd)

    ti = lax.broadcasted_iota(jnp.int32, (tt, tt), 0)
    si = lax.broadcasted_iota(jnp.int32, (tt, tt), 1)
    same_chunk_lower = ((ti // RW_CHUNK) == (si // RW_CHUNK)) & (si <= ti)
    cum = _split_dot_left(jnp.where(same_chunk_lower, 1.0, 0.0).astype(BF16), log_decay)
    p_in = jnp.exp(cum)
    p_ex = jnp.exp(cum - log_decay)
    p_inv = jnp.exp(-cum)
    a_t = -kk * p_ex
    b_t = kk * a * p_inv
    k_t = k2 * p_inv
    r_t = r * p_in
    for h in range(B_HEADS):
        lanes = slice(h * HEAD_DIM, (h + 1) * HEAD_DIM)
        cols = [z[:, lanes] for z in (a_t, r_t, b_t, k_t, v, p_in)]
        for c in range(n_chunks):
            rows = slice(c * RW_CHUNK, (c + 1) * RW_CHUNK)
            n = c * B_HEADS + h
            at_ref[n] = cols[0][rows].astype(BF16)
            rt_ref[n] = cols[1][rows]
            bt_ref[n] = cols[2][rows].astype(BF16)
            kt_ref[n] = cols[3][rows].astype(BF16)
            vv_ref[n] = cols[4][rows].astype(BF16)
            pc_ref[n] = cols[5][(c + 1) * RW_CHUNK - 1:(c + 1) * RW_CHUNK]
    post_ref[0] = _head_sum(r * k2 * rk_ref[...], bd) * v
    post_ref[1] = g

    ci = lax.broadcasted_iota(jnp.int32, (RW_CHUNK, RW_CHUNK), 0)
    cj = lax.broadcasted_iota(jnp.int32, (RW_CHUNK, RW_CHUNK), 1)
    ci2 = lax.broadcasted_iota(jnp.int32, (RW_CHUNK, 2 * RW_CHUNK), 0)
    cj2 = lax.broadcasted_iota(jnp.int32, (RW_CHUNK, 2 * RW_CHUNK), 1) % RW_CHUNK
    strict = (cj2 < ci2)[None]
    incl = (cj2 <= ci2)[None]
    eye = jnp.where(ci == cj, 1.0, 0.0)[None]

    at = at_ref[...]
    rt = rt_ref[...]
    bt = bt_ref[...]
    kt = kt_ref[...]
    vv = vv_ref[...]
    bk = jnp.concatenate([bt, kt], axis=1)
    aa = _bmm_nt(jnp.concatenate([at, rt.astype(BF16)], axis=1), bk)
    a_a = jnp.where(strict, aa[:, :RW_CHUNK], 0.0)
    a_r = jnp.where(incl, aa[:, RW_CHUNK:], 0.0).astype(BF16)
    a_ab = a_a[:, :, :RW_CHUNK]
    a_ak = a_a[:, :, RW_CHUNK:].astype(BF16)
    a_rb = a_r[:, :, :RW_CHUNK]
    t_inv = eye + a_ab
    n_pow = a_ab
    for _ in range(5):
        n_b = n_pow.astype(BF16)
        n_pow = _bmm(n_b, n_b)
        t_inv = t_inv + _bmm(n_pow.astype(BF16), t_inv.astype(BF16))
    t_b = t_inv.astype(BF16)
    w_m = _bmm(t_b, at).astype(BF16)
    u0 = _bmm(t_b, _bmm(a_ak, vv).astype(BF16)).astype(BF16)
    uv = jnp.concatenate([u0, vv], axis=1)
    m_p = _bmm_tn(bt, w_m).astype(BF16)
    g_t = _bmm_tn(uv, bk)
    r_p = (rt + _bmm(a_rb, w_m)).astype(BF16)
    o_loc = _bmm(a_r, uv)

    s = s_ref[0]
    for c in range(n_chunks):
        tiles = slice(c * B_HEADS, (c + 1) * B_HEADS)
        sb = s.astype(BF16)
        o_c = _bmm_nt(r_p[tiles], sb) + o_loc[tiles]
        s = (s + _bmm_nt(sb, m_p[tiles]) + g_t[tiles]) * pc_ref[tiles]
        o_c = o_c - jnp.mean(o_c, axis=-1, keepdims=True)
        o_c = o_c * lax.rsqrt(jnp.mean(o_c * o_c, axis=-1, keepdims=True) + GN_EPS)
        for h in range(B_HEADS):
            o_ref[c * RW_CHUNK:(c + 1) * RW_CHUNK, h * HEAD_DIM:(h + 1) * HEAD_DIM] = o_c[h]
    s_ref[0] = s

    yn = o_ref[...] * lng_ref[...] + lnb_ref[...]
    y_ref[...] = ((yn + post_ref[0]) * post_ref[1]).astype(BF16)


def _split_dot_left(m, x):
    hi = x.astype(BF16)
    lo = (x - hi.astype(F32)).astype(BF16)
    return _dot(m, hi) + _dot(m, lo)


def _rwkv_prompt(pb, params, *, batch, seq, tt):
    m = pb.shape[0]
    nt = seq // tt
    in_specs = [pl.BlockSpec((tt, B_COLS), lambda b, t: (b * nt + t, 0))] + [_const_spec(p.shape) for p in params]
    n_tiles = B_HEADS * tt // RW_CHUNK
    head_scratch = lambda dt: pltpu.VMEM((n_tiles, RW_CHUNK, HEAD_DIM), dt)
    return pl.pallas_call(
        _rwkv_prompt_kernel,
        grid=(batch, nt),
        in_specs=in_specs,
        out_specs=[pl.BlockSpec((tt, B_WIDTH), lambda b, t: (b * nt + t, 0)),
                   pl.BlockSpec((1, B_HEADS, HEAD_DIM, HEAD_DIM), lambda b, t: (b, 0, 0, 0))],
        out_shape=[jax.ShapeDtypeStruct((m, B_WIDTH), BF16),
                   jax.ShapeDtypeStruct((batch, B_HEADS, HEAD_DIM, HEAD_DIM), F32)],
        scratch_shapes=[pltpu.VMEM((8, B_COLS), F32), head_scratch(BF16), head_scratch(F32), head_scratch(BF16),
                        head_scratch(BF16), head_scratch(BF16), pltpu.VMEM((n_tiles, 1, HEAD_DIM), F32),
                        pltpu.VMEM((tt, B_WIDTH), F32), pltpu.VMEM((2, tt, B_WIDTH), F32)],
        compiler_params=pltpu.CompilerParams(dimension_semantics=("arbitrary", "arbitrary"),
                                             vmem_limit_bytes=VMEM_LIMIT),
        name="rwkv_prompt",
    )(pb, *params)


def _rwkv_step_kernel(first_layer, pb_ref, sh_ref, s_ref, mu_ref, w0_ref, w2_ref, a0_ref, a2_ref, g2_ref, kk_ref,
                      ka_ref, rk_ref, lng_ref, lnb_ref, bd_ref, *rest):
    if first_layer:
        y_ref, so_ref, row_ref, col_ref, yt_ref = rest
    else:
        _, y_ref, so_ref, row_ref, col_ref, yt_ref = rest
    h = pl.program_id(0)
    bd = bd_ref[...]

    @pl.when(h == 0)
    def _():
        r, k2, v, kk, a, log_decay, g = _rwkv_prep(pb_ref[...], sh_ref[...], mu_ref[...], w0_ref[...], w2_ref[...],
                                                   a0_ref[...], a2_ref[...], g2_ref[...], kk_ref[...], ka_ref[...], bd)
        for n, x in enumerate((r, k2, v, g)):
            row_ref[n] = x
        for n, x in enumerate((r, jnp.exp(log_decay), k2, v, kk, kk * a)):
            col_ref[n] = x.T

    rows = pl.ds(pl.multiple_of(h * HEAD_DIM, HEAD_DIM), HEAD_DIM)
    r_c, w_c, k_c, v_c, kk_c, b_c = (col_ref[n, rows, :] for n in range(6))
    for i in range(HEAD_DIM):
        s = s_ref[0, 0, i]
        s_kk = jnp.sum(s * kk_c, axis=0, keepdims=True)
        s_new = s * w_c - s_kk * b_c + v_c[i:i + 1] * k_c
        so_ref[0, 0, i] = s_new
        yt_ref[pl.ds(h * HEAD_DIM + i, 1), :] = jnp.sum(s_new * r_c, axis=0, keepdims=True)

    @pl.when(h == pl.num_programs(0) - 1)
    def _():
        y_ref[...] = _rwkv_post(yt_ref[...].T, row_ref[0], row_ref[1], row_ref[2], row_ref[3], rk_ref[...],
                                lng_ref[...], lnb_ref[...], bd).astype(BF16)


def _rwkv_step(pb, shift, state_t, states_out, params, layer):
    m = pb.shape[0]
    st_spec = pl.BlockSpec((1, 1, HEAD_DIM, HEAD_DIM, m), lambda h: (layer, h, 0, 0, 0))
    full = lambda width: pl.BlockSpec((m, width), lambda h: (0, 0))
    in_specs = [full(B_COLS), pl.BlockSpec((None, m, B_COLS), lambda h: (layer, 0, 0)), st_spec]
    in_specs += [_const_spec(p.shape) for p in params]
    args = [pb, shift, state_t, *params]
    aliases = {}
    if states_out is not None:
        in_specs.append(pl.BlockSpec(memory_space=pl.ANY))
        aliases = {len(args): 1}
        args.append(states_out)
    return pl.pallas_call(
        functools.partial(_rwkv_step_kernel, states_out is None),
        grid=(B_HEADS,),
        in_specs=in_specs,
        out_specs=[full(B_WIDTH), st_spec],
        out_shape=[jax.ShapeDtypeStruct((m, B_WIDTH), BF16), jax.ShapeDtypeStruct(state_t.shape, F32)],
        input_output_aliases=aliases,
        scratch_shapes=[pltpu.VMEM((4, m, B_WIDTH), F32), pltpu.VMEM((6, B_WIDTH, m), F32),
                        pltpu.VMEM((B_WIDTH, m), F32)],
        compiler_params=pltpu.CompilerParams(dimension_semantics=("arbitrary",), vmem_limit_bytes=VMEM_LIMIT),
        name="rwkv_sample",
    )(*args)


def _sb_chains(z_chains, mask_chains, carries, tri):
    units = [(c, n) for c in range(len(z_chains)) for n in range(len(z_chains[c]))]
    log_beta, log_keep, after, att = {}, {}, {}, {}
    for c, n in units:
        z = z_chains[c][n]
        log_beta[c, n] = jnp.minimum(z, 0.0) - jnp.log(1.0 + jnp.exp(-jnp.abs(z)))
        keep = log_beta[c, n] - z
        log_keep[c, n] = keep if mask_chains[c][n] is None else jnp.where(mask_chains[c][n], keep, 0.0)
    for u in units:
        after[u] = _dot(log_keep[u].astype(BF16), tri)
    carries = list(carries)
    keys = tri.shape[0]
    for c, n in units:
        w = jnp.exp(log_beta[c, n] + after[c, n][:, :keys] + carries[c])
        att[c, n] = (w if mask_chains[c][n] is None else jnp.where(mask_chains[c][n], w, 0.0)).astype(BF16)
        carries[c] = carries[c] + after[c, n][:, keys:]
    return [[att[c, n] for n in range(len(z_chains[c]))] for c in range(len(z_chains))], carries


def _suffix_and_total(n):
    later = lax.broadcasted_iota(jnp.int32, (n, 2 * n), 0) > lax.broadcasted_iota(jnp.int32, (n, 2 * n), 1)
    whole = lax.broadcasted_iota(jnp.int32, (n, 2 * n), 1) >= n
    return jnp.where(later | whole, 1.0, 0.0).astype(BF16)


def _sb_prompt_kernel(bias_ref, q_ref, k_ref, v_ref, o_ref, acc_ref, carry_ref):
    blk = SB_BLOCK
    n_sub = q_ref.shape[1] // blk
    tri = _suffix_and_total(blk)
    causal = lax.broadcasted_iota(jnp.int32, (blk, blk), 1) < lax.broadcasted_iota(jnp.int32, (blk, blk), 0)

    def query_block(sub, _):
        i = pl.program_id(1) * n_sub + sub
        q_rows = pl.ds(pl.multiple_of(sub * blk, blk), blk)
        acc_ref[...] = jnp.zeros_like(acc_ref)
        carry_ref[...] = jnp.zeros_like(carry_ref)

        def process(blocks, masks):
            offs = [pl.multiple_of(j * blk, blk) for j in blocks]
            zs = [[_dot_nt(q_ref[h, q_rows, :], k_ref[h, pl.ds(off, blk), :]) + bias_ref[h] for off in offs]
                  for h in range(C_HEADS)]
            att, carry = _sb_chains(zs, [masks] * C_HEADS, [carry_ref[h] for h in range(C_HEADS)], tri)
            pv = [[_dot(att[h][n], v_ref[h, pl.ds(off, blk), :]) for n, off in enumerate(offs)]
                  for h in range(C_HEADS)]
            for h in range(C_HEADS):
                acc_ref[h] += functools.reduce(lambda a, b: a + b, pv[h])
                carry_ref[h] = carry[h]

        for rem in range(SB_GROUP):
            @pl.when(i % SB_GROUP == rem)
            def _():
                process([i - n for n in range(rem + 1)], [causal] + [None] * rem)

        def group(p, _):
            first = i - i % SB_GROUP - 1 - SB_GROUP * p
            process([first - n for n in range(SB_GROUP)], [None] * SB_GROUP)
            return 0

        lax.fori_loop(0, i // SB_GROUP, group, 0)
        for h in range(C_HEADS):
            o_ref[q_rows, h * HEAD_DIM:(h + 1) * HEAD_DIM] = acc_ref[h].astype(BF16)
        return 0

    lax.fori_loop(0, n_sub, query_block, 0)


def _sb_prompt(bias, qh, kh, vh, *, batch, seq, n_sub):
    m = qh.shape[1]
    steps = seq // (SB_BLOCK * n_sub)
    rows = SB_BLOCK * n_sub
    kv_spec = pl.BlockSpec((C_HEADS, seq, HEAD_DIM), lambda b, i: (0, b, 0))
    return pl.pallas_call(
        _sb_prompt_kernel,
        grid=(batch, steps),
        in_specs=[pl.BlockSpec(memory_space=pltpu.SMEM),
                  pl.BlockSpec((C_HEADS, rows, HEAD_DIM), lambda b, i: (0, b * steps + i, 0)), kv_spec, kv_spec],
        out_specs=pl.BlockSpec((rows, C_WIDTH), lambda b, i: (b * steps + i, 0)),
        out_shape=jax.ShapeDtypeStruct((m, C_WIDTH), BF16),
        scratch_shapes=[pltpu.VMEM((C_HEADS, SB_BLOCK, HEAD_DIM), F32), pltpu.VMEM((C_HEADS, SB_BLOCK, SB_BLOCK), F32)],
        compiler_params=pltpu.CompilerParams(dimension_semantics=("arbitrary", "arbitrary"),
                                             vmem_limit_bytes=VMEM_LIMIT),
        name="sb_prompt",
    )(bias, qh, kh, vh)


def _sb_sample_kernel(n_seq, n_pages, page_size, pt_ref, bias_ref, q_ref, kn_ref, vn_ref, *refs):
    n_blocks = n_seq * n_pages
    k_refs = refs[:n_blocks]
    v_refs = refs[n_blocks:2 * n_blocks]
    o_ref = refs[2 * n_blocks]
    rows = 8
    tri = _suffix_and_total(page_size)
    r_col = lax.broadcasted_iota(jnp.int32, (rows, 1), 0)
    bias = jnp.zeros((rows, 1), F32)
    for h in range(C_HEADS):
        bias = jnp.where(r_col == h, bias_ref[h], bias)
    head_lanes = (lax.broadcasted_iota(jnp.int32, (rows, C_WIDTH), 1) // HEAD_DIM
                  == lax.broadcasted_iota(jnp.int32, (rows, C_WIDTH), 0))
    past = n_pages * page_size
    new_is_causal = (past + 0) < (past + 0)
    pages = list(reversed(range(n_pages)))

    z_chains, carries, w_new = [], [], []
    for s in range(n_seq):
        q = jnp.where(head_lanes, q_ref[s], 0.0)
        q_b = q.astype(BF16)
        z_new = jnp.sum(q * kn_ref[s], axis=-1, keepdims=True) + bias
        carries.append(jnp.broadcast_to(jnp.where(new_is_causal, -_softplus(z_new), 0.0), (rows, page_size)))
        w_new.append(jnp.where(new_is_causal, jnp.exp(-_softplus(-z_new)), 0.0))
        z_chains.append([_dot(q_b, k_refs[s * n_pages + p][0, 0].astype(BF16)) + bias for p in pages])
    att, _ = _sb_chains(z_chains, [[None] * n_pages] * n_seq, carries, tri)
    for s in range(n_seq):
        pv = [_dot_nt(att[s][n], v_refs[s * n_pages + p][0, 0].astype(BF16)) for n, p in enumerate(pages)]
        out = w_new[s] * vn_ref[s] + functools.reduce(lambda a, b: a + b, pv)
        o_ref[s] = jnp.sum(jnp.where(head_lanes, out, 0.0), axis=0, keepdims=True).astype(BF16)


def _sb_sample(page_table, bias, q, k_new, v_new, cache_k, cache_v, layer, *, n_seq):
    m = q.shape[0]
    n_pages = page_table.shape[1]
    page_size = cache_k.shape[2]
    to_pages = lambda c: jnp.transpose(c, (0, 1, 3, 4, 2)).reshape(c.shape[0], c.shape[1], C_WIDTH, page_size)
    cache_k, cache_v = to_pages(cache_k), to_pages(cache_v)
    pt = page_table.reshape(-1)
    row3 = lambda x: x.reshape(m, 1, C_WIDTH)
    row_spec = pl.BlockSpec((n_seq, 1, C_WIDTH), lambda b, pt: (b, 0, 0))

    def page_spec(s, p):
        return pl.BlockSpec((1, 1, C_WIDTH, page_size),
                            lambda b, pt: (layer, pt[(b * n_seq + s) * n_pages + p], 0, 0))

    page_specs = [page_spec(s, p) for s in range(n_seq) for p in range(n_pages)]
    grid_spec = pltpu.PrefetchScalarGridSpec(
        num_scalar_prefetch=1,
        grid=(m // n_seq,),
        in_specs=[pl.BlockSpec(memory_space=pltpu.SMEM), row_spec, row_spec, row_spec] + page_specs + page_specs,
        out_specs=row_spec,
    )
    n_blocks = n_seq * n_pages
    out = pl.pallas_call(
        functools.partial(_sb_sample_kernel, n_seq, n_pages, page_size),
        grid_spec=grid_spec,
        out_shape=jax.ShapeDtypeStruct((m, 1, C_WIDTH), BF16),
        compiler_params=pltpu.CompilerParams(dimension_semantics=("arbitrary",), vmem_limit_bytes=VMEM_LIMIT),
        name="sb_sample",
    )(pt, bias, row3(q), row3(k_new), row3(v_new), *([cache_k] * n_blocks), *([cache_v] * n_blocks))
    return out.reshape(m, C_WIDTH)


def _tail_kernel(h_ref, ya_ref, yb_ref, yc_ref, pe_ref, wo_ref, fg_ref, wg_ref, wu_ref, wd_ref, pg_ref, wpg_ref,
                 wple_ref, out_ref):
    h = (h_ref[...] + _dot(ya_ref[...], wo_ref[0:A_WIDTH, :]) + _dot(yb_ref[...], wo_ref[A_WIDTH:A_WIDTH + B_WIDTH, :])
         + _dot(yc_ref[...], wo_ref[A_WIDTH + B_WIDTH:, :]))
    hn = _rms(h, fg_ref[...]).astype(BF16)
    gate = _dot(hn, wg_ref[...])
    act = (gate * jax.nn.sigmoid(gate) * _dot(hn, wu_ref[...])).astype(BF16)
    h = h + _dot(act, wd_ref[...])
    gate = jax.nn.sigmoid(_dot(_rms(h, pg_ref[...]).astype(BF16), wpg_ref[...]))
    out_ref[...] = h + _dot(pe_ref[...].astype(BF16), wple_ref[...]) * gate


def _tail(h, ya, yb, yc, pe, wo, fg, wg, wu, wd, pg, wpg, wple, *, tm, layer):
    m, d = h.shape
    row = lambda width: pl.BlockSpec((tm, width), lambda i: (i, 0))
    weights = (wo, fg, wg, wu, wd, pg, wpg, wple)
    return pl.pallas_call(
        _tail_kernel,
        grid=(m // tm,),
        in_specs=[row(d), row(A_WIDTH), row(B_WIDTH), row(C_WIDTH),
                  pl.BlockSpec((None, tm, pe.shape[2]), lambda i: (layer, i, 0))]
        + [_const_spec(w.shape) for w in weights],
        out_specs=row(d),
        out_shape=jax.ShapeDtypeStruct((m, d), F32),
        compiler_params=pltpu.CompilerParams(dimension_semantics=("arbitrary",), vmem_limit_bytes=VMEM_LIMIT),
        name="tail",
    )(h, ya, yb, yc, pe, *weights)


def _pad_rows(w, top, total):
    return jnp.pad(w, ((top, total - top - w.shape[0]), (0, 0)))


def kernel(x_prompt, x_sample, cache_k, cache_v, state_wkv, state_shift, page_table, p_prompt, p_sample, mix_norm, w_in, a_ln_g, a_ln_b, a_ws, a_bs, b_mu, b_w0, b_w2, b_a0, b_a2, b_g2, b_kk, b_ka, b_rk, b_ln_g, b_ln_b, c_qn, c_kn, c_bias, w_out, ffn_norm, w_gate, w_up, w_down, ple_norm, w_ple_gate, w_ple):
    depth = w_in.shape[0]
    bp, seq, d = x_prompt.shape
    bs = x_sample.shape[0]
    hp = x_prompt.reshape(bp * seq, d)
    hs = x_sample.reshape(bs, d)
    hd = lax.broadcasted_iota(jnp.int32, (B_WIDTH, B_WIDTH), 0) // HEAD_DIM
    bd = (hd == hd.T).astype(BF16)
    row = lambda x: x.reshape(1, -1)

    outs = {n: [] for n in ("wkvp", "shp", "ks", "vs", "shs", "avs")}
    pe_p = p_prompt.reshape(depth, bp * seq, -1)
    pe_s = p_sample.reshape(depth, bs, -1)
    state_t = jnp.transpose(state_wkv, (0, 2, 3, 4, 1))
    kv_shape = (depth, bp, C_HEADS, HEAD_DIM, seq)
    kv_bufs = None
    states_out = None
    for i in range(depth):
        w_in_b = w_in[i].astype(BF16)
        proj_common = (row(mix_norm[i]), w_in_b, row(a_ln_g[i]), row(a_ln_b[i]))
        bias_lanes = jnp.repeat(a_bs[i].T, HEAD_DIM, axis=1)
        w00_lanes = jnp.repeat(a_ws[i][:, 0, 0], HEAD_DIM).reshape(1, A_WIDTH)
        rw_params = (row(b_mu[i]), row(b_w0[i]), _pad_rows(b_w2[i], 0, LORA_WA).astype(BF16), row(b_a0[i]),
                     _pad_rows(b_a2[i], LORA_WA - b_a2.shape[1], LORA_WA).astype(BF16), b_g2[i].astype(BF16),
                     row(b_kk[i]), row(b_ka[i]), row(b_rk[i]), row(b_ln_g[i]), row(b_ln_b[i]), bd)
        tail_w = (w_out[i].astype(BF16), row(ffn_norm[i]), w_gate[i].astype(BF16), w_up[i].astype(BF16),
                  w_down[i].astype(BF16), row(ple_norm[i]), w_ple_gate[i].astype(BF16), w_ple[i].astype(BF16))
        qk_norm = (row(jnp.tile(c_qn[i], C_HEADS)), row(jnp.tile(c_kn[i], C_HEADS)), bd[:C_WIDTH, :C_WIDTH])

        ya, pb, qh, kh, vh, *kv_bufs = _proj(hp, *proj_common, a_ws[i], bias_lanes, *qk_norm, sample=False, tm=512,
                                             layer=i, seq=seq, kv_shape=kv_shape, kv_bufs=kv_bufs)
        yb, s_fin = _rwkv_prompt(pb, rw_params, batch=bp, seq=seq, tt=256)
        yc = _sb_prompt(c_bias[i], qh, kh, vh, batch=bp, seq=seq, n_sub=4)
        hp = _tail(hp, ya, yb, yc, pe_p, *tail_w, tm=512, layer=i)
        outs["wkvp"].append(s_fin)
        outs["shp"].append(pb.reshape(bp, seq, B_COLS)[:, -1])

        ya, pb, q_n, k_n, v_n, a_vn = _proj(hs, *proj_common, w00_lanes, bias_lanes[0:1], *qk_norm, sample=True, tm=bs)
        yb, states_out = _rwkv_step(pb, state_shift, state_t, states_out, rw_params, i)
        yc = _sb_sample(page_table, c_bias[i], q_n, k_n, v_n, cache_k, cache_v, i, n_seq=4)
        hs = _tail(hs, ya, yb, yc, pe_s, *tail_w, tm=bs, layer=i)
        outs["ks"].append(k_n.reshape(bs, 1, C_HEADS, HEAD_DIM))
        outs["vs"].append(v_n.reshape(bs, 1, C_HEADS, HEAD_DIM))
        outs["shs"].append(pb)
        outs["avs"].append(a_vn.reshape(bs, 1, A_WIDTH))

    st = {n: jnp.stack(v) for n, v in outs.items()}
    k_prompt, v_prompt = (jnp.transpose(x, (0, 1, 4, 2, 3)) for x in kv_bufs)
    wkv_sample = jnp.transpose(states_out, (0, 4, 1, 2, 3))
    return (hp.reshape(bp, seq, d), hs.reshape(bs, 1, d), k_prompt, v_prompt, st["wkvp"], st["shp"],
            st["ks"], st["vs"], wkv_sample, st["shs"], st["avs"])
```

```python
import functools

import jax
import jax.numpy as jnp
from jax import lax
from jax.experimental import pallas as pl
from jax.experimental.pallas import tpu as pltpu

F32 = jnp.float32
BF16 = jnp.bfloat16

HEAD_DIM = 64
A_WIDTH = 256
A_GROUPS = 4
CHUNK = 128
B_WIDTH = 512
B_HEADS = 8
B_COLS = 1792
LORA_WA = 128
C_WIDTH = 256
C_HEADS = 4
SB_BLOCK = 128
SB_GROUP = 8
SB_SCALE = HEAD_DIM ** -0.5
RMS_EPS = 1e-6
LN_EPS = 1e-5
GN_EPS = 64e-5
RW_CHUNK = 64

VMEM_LIMIT = 56 * 1024 * 1024


def _dot(a, b):
    return jnp.dot(a, b, preferred_element_type=F32)


def _dot_nt(a, b):
    return lax.dot_general(a, b, (((1,), (1,)), ((), ())), preferred_element_type=F32)


def _rms(x, g):
    return x * lax.rsqrt(jnp.mean(x * x, axis=-1, keepdims=True) + RMS_EPS) * g


def _softplus(x):
    return jnp.maximum(x, 0.0) + jnp.log(1.0 + jnp.exp(-jnp.abs(x)))


def _const_spec(shape):
    nd = len(shape)
    return pl.BlockSpec(shape, lambda *_: (0,) * nd, pipeline_mode=pl.Buffered(1))


def _proj_kernel(sample, h_ref, g_ref, w_ref, lng_ref, lnb_ref, ws_ref, bs_ref, qn_ref, kn_ref, bd_ref, *rest):
    if sample:
        ya_ref, pb_ref, q_ref, k_ref, v_ref, av_ref = rest
    else:
        ya_ref, pb_ref, qh_ref, kh_ref, vh_ref, kt_ref, vt_ref = rest[-7:]
    tm = h_ref.shape[0]
    hn = _rms(h_ref[...], g_ref[...]).astype(BF16)

    pc = _dot(hn, w_ref[:, 2 * A_WIDTH + B_COLS:])
    q = pc[:, :C_WIDTH]
    k = pc[:, C_WIDTH:2 * C_WIDTH]
    v = pc[:, 2 * C_WIDTH:]
    bd = bd_ref[...]
    q = q * lax.rsqrt(_head_sum(q * q, bd) * (1.0 / HEAD_DIM) + RMS_EPS) * (qn_ref[...] * SB_SCALE)
    k = k * lax.rsqrt(_head_sum(k * k, bd) * (1.0 / HEAD_DIM) + RMS_EPS) * kn_ref[...]
    if sample:
        q_ref[...] = q
        k_ref[...] = k
        v_ref[...] = v
    else:
        for dst, x in ((qh_ref, q.astype(BF16)), (kh_ref, k.astype(BF16)), (vh_ref, v.astype(BF16))):
            for h in range(C_HEADS):
                dst[h] = x[:, h * HEAD_DIM:(h + 1) * HEAD_DIM]
        for dst, x in ((kt_ref, k.T), (vt_ref, v.T)):
            for h in range(C_HEADS):
                dst[0, 0, h] = x[h * HEAD_DIM:(h + 1) * HEAD_DIM, :]

    pa = _dot(hn, w_ref[:, 0:2 * A_WIDTH])
    a_u = pa[:, :A_WIDTH]
    a_v = pa[:, A_WIDTH:]
    xc = a_v - jnp.mean(a_v, axis=-1, keepdims=True)
    a_vn = xc * lax.rsqrt(jnp.mean(xc * xc, axis=-1, keepdims=True) + LN_EPS) * lng_ref[...] + lnb_ref[...]
    if sample:
        av_ref[...] = a_vn
        ya_ref[...] = (a_u * (ws_ref[...] * a_vn + bs_ref[...])).astype(BF16)
    else:
        causal = lax.broadcasted_iota(jnp.int32, (CHUNK, CHUNK), 0) >= lax.broadcasted_iota(jnp.int32, (CHUNK, CHUNK), 1)
        group = lax.broadcasted_iota(jnp.int32, (CHUNK, A_WIDTH), 1) // HEAD_DIM
        w_tril = [jnp.where(causal, ws_ref[g], 0.0).astype(BF16) for g in range(A_GROUPS)]
        for c in range(tm // CHUNK):
            rows = slice(c * CHUNK, (c + 1) * CHUNK)
            v_c = a_vn[rows]
            mixed = bs_ref[...]
            for g in range(A_GROUPS):
                mixed = mixed + _dot(w_tril[g], jnp.where(group == g, v_c, 0.0).astype(BF16))
            ya_ref[rows, :] = (a_u[rows] * mixed).astype(BF16)

    pb_ref[...] = _dot(hn, w_ref[:, 2 * A_WIDTH:2 * A_WIDTH + B_COLS])


def _proj(h, g, w, lng, lnb, ws, bs, qn, kn, bd, *, sample, tm, layer=0, seq=None, kv_shape=None, kv_bufs=None):
    m, d = h.shape
    row = lambda width: pl.BlockSpec((tm, width), lambda i: (i, 0))
    head = pl.BlockSpec((C_HEADS, tm, HEAD_DIM), lambda i: (0, i, 0))
    in_specs = [row(d), _const_spec(g.shape), _const_spec(w.shape), _const_spec(lng.shape), _const_spec(lnb.shape),
                _const_spec(ws.shape), _const_spec(bs.shape), _const_spec(qn.shape), _const_spec(kn.shape),
                _const_spec(bd.shape)]
    args = [h, g, w, lng, lnb, ws, bs, qn, kn, bd]
    aliases = {}
    if sample:
        out_shape = [jax.ShapeDtypeStruct((m, A_WIDTH), BF16), jax.ShapeDtypeStruct((m, B_COLS), F32),
                     jax.ShapeDtypeStruct((m, C_WIDTH), F32), jax.ShapeDtypeStruct((m, C_WIDTH), F32),
                     jax.ShapeDtypeStruct((m, C_WIDTH), F32), jax.ShapeDtypeStruct((m, A_WIDTH), F32)]
        out_specs = [row(A_WIDTH), row(B_COLS), row(C_WIDTH), row(C_WIDTH), row(C_WIDTH), row(A_WIDTH)]
    else:
        tiles = seq // tm
        hshape = jax.ShapeDtypeStruct((C_HEADS, m, HEAD_DIM), BF16)
        kv = jax.ShapeDtypeStruct(kv_shape, F32)
        kv_spec = pl.BlockSpec((1, 1, C_HEADS, HEAD_DIM, tm), lambda i: (layer, i // tiles, 0, 0, i % tiles))
        out_shape = [jax.ShapeDtypeStruct((m, A_WIDTH), BF16), jax.ShapeDtypeStruct((m, B_COLS), F32),
                     hshape, hshape, hshape, kv, kv]
        out_specs = [row(A_WIDTH), row(B_COLS), head, head, head, kv_spec, kv_spec]
        if kv_bufs is not None:
            in_specs += [pl.BlockSpec(memory_space=pl.ANY)] * 2
            aliases = {len(args): 5, len(args) + 1: 6}
            args += list(kv_bufs)
    return pl.pallas_call(
        functools.partial(_proj_kernel, sample),
        grid=(m // tm,),
        in_specs=in_specs,
        out_specs=out_specs,
        out_shape=out_shape,
        input_output_aliases=aliases,
        compiler_params=pltpu.CompilerParams(dimension_semantics=("arbitrary",), vmem_limit_bytes=VMEM_LIMIT),
        name="proj_sample" if sample else "proj_prompt",
    )(*args)


def _head_sum(x, bd):
    return _dot(x.astype(BF16), bd)


def _rwkv_prep(x, prev, mu, w0, w2p, a0, a2p, g2, k_k, k_a, bd):
    xs = x + (prev - x) * mu
    r = xs[:, 0:B_WIDTH]
    k = xs[:, B_WIDTH:2 * B_WIDTH]
    v = xs[:, 2 * B_WIDTH:3 * B_WIDTH]
    wa = xs[:, 3 * B_WIDTH:3 * B_WIDTH + LORA_WA]
    gl = xs[:, 3 * B_WIDTH + LORA_WA:]
    w_log = -_softplus(-(w0 + _dot(jnp.tanh(wa).astype(BF16), w2p))) - 0.5
    log_decay = -jnp.exp(w_log)
    a = jax.nn.sigmoid(a0 + _dot(wa.astype(BF16), a2p))
    g = _dot(jax.nn.sigmoid(gl).astype(BF16), g2)
    kk = k * k_k
    kk = kk * lax.rsqrt(_head_sum(kk * kk, bd) + 1e-12)
    k2 = k * (1.0 + (a - 1.0) * k_a)
    return r, k2, v, kk, a, log_decay, g


def _rwkv_post(y, r, k2, v, g, r_k, ln_g, ln_b, bd):
    yc = y - _head_sum(y, bd) * (1.0 / HEAD_DIM)
    yn = yc * lax.rsqrt(_head_sum(yc * yc, bd) * (1.0 / HEAD_DIM) + GN_EPS) * ln_g + ln_b
    bonus = _head_sum(r * k2 * r_k, bd) * v
    return (yn + bonus) * g


def _bmm(a, b):
    return lax.dot_general(a, b, (((2,), (1,)), ((0,), (0,))), preferred_element_type=F32)


def _bmm_nt(a, b):
    return lax.dot_general(a, b, (((2,), (2,)), ((0,), (0,))), preferred_element_type=F32)


def _bmm_tn(a, b):
    return lax.dot_general(a, b, (((1,), (1,)), ((0,), (0,))), preferred_element_type=F32)


def _rwkv_prompt_kernel(pb_ref, mu_ref, w0_ref, w2_ref, a0_ref, a2_ref, g2_ref, kk_ref, ka_ref, rk_ref, lng_ref,
                        lnb_ref, bd_ref, y_ref, s_ref,
                        prev_ref, at_ref, rt_ref, bt_ref, kt_ref, vv_ref, pc_ref, o_ref, post_ref):
    tt = pb_ref.shape[0]
    n_chunks = tt // RW_CHUNK
    t_idx = pl.program_id(1)

    @pl.when(t_idx == 0)
    def _():
        prev_ref[...] = jnp.zeros_like(prev_ref)
        s_ref[...] = jnp.zeros_like(s_ref)

    x = pb_ref[...]
    row_id = lax.broadcasted_iota(jnp.int32, x.shape, 0)
    prev = jnp.where(row_id == 0, prev_ref[0:1, :], pltpu.roll(x, 1, axis=0))
    prev_ref[0:1, :] = x[tt - 1:tt, :]
    bd = bd_ref[...]
    r, k2, v, kk, a, log_decay, g = _rwkv_prep(x, prev, mu_ref[...], w0_ref[...], w2_ref[...], a0_ref[...],
                                               a2_ref[...], g2_ref[...], kk_ref[...], ka_ref[...], bd)

    ti = lax.broadcasted_iota(jnp.int32, (tt, tt), 0)
    si = lax.broadcasted_iota(jnp.int32, (tt, tt), 1)
    same_chunk_lower = ((ti // RW_CHUNK) == (si // RW_CHUNK)) & (si <= ti)
    cum = _split_dot_left(jnp.where(same_chunk_lower, 1.0, 0.0).astype(BF16), log_decay)
    p_in = jnp.exp(cum)
    p_ex = jnp.exp(cum - log_decay)
    p_inv = jnp.exp(-cum)
    a_t = -kk * p_ex
    b_t = kk * a * p_inv
    k_t = k2 * p_inv
    r_t = r * p_in
    for h in range(B_HEADS):
        lanes = slice(h * HEAD_DIM, (h + 1) * HEAD_DIM)
        cols = [z[:, lanes] for z in (a_t, r_t, b_t, k_t, v, p_in)]
        for c in range(n_chunks):
            rows = slice(c * RW_CHUNK, (c + 1) * RW_CHUNK)
            n = c * B_HEADS + h
            at_ref[n] = cols[0][rows].astype(BF16)
            rt_ref[n] = cols[1][rows]
            bt_ref[n] = cols[2][rows].astype(BF16)
            kt_ref[n] = cols[3][rows].astype(BF16)
            vv_ref[n] = cols[4][rows].astype(BF16)
            pc_ref[n] = cols[5][(c + 1) * RW_CHUNK - 1:(c + 1) * RW_CHUNK]
    post_ref[0] = _head_sum(r * k2 * rk_ref[...], bd) * v
    post_ref[1] = g

    ci2 = lax.broadcasted_iota(jnp.int32, (RW_CHUNK, 2 * RW_CHUNK), 0)
    cj2 = lax.broadcasted_iota(jnp.int32, (RW_CHUNK, 2 * RW_CHUNK), 1) % RW_CHUNK
    strict = (cj2 < ci2)[None]
    incl = (cj2 <= ci2)[None]
    eye2 = jnp.where(cj2 == ci2, 1.0, 0.0)[None]

    at = at_ref[...]
    rt = rt_ref[...]
    bt = bt_ref[...]
    kt = kt_ref[...]
    vv = vv_ref[...]
    bk = jnp.concatenate([bt, kt], axis=1)
    aa = _bmm_nt(jnp.concatenate([at, rt.astype(BF16)], axis=1), bk)
    a_a = jnp.where(strict, aa[:, :RW_CHUNK], 0.0)
    a_r = jnp.where(incl, aa[:, RW_CHUNK:], 0.0).astype(BF16)
    a_ak = a_a[:, :, RW_CHUNK:].astype(BF16)
    a_rb = a_r[:, :, :RW_CHUNK]
    right = lax.broadcasted_iota(jnp.int32, (1, RW_CHUNK, 2 * RW_CHUNK), 2) >= RW_CHUNK
    nt = jnp.where(right, eye2, a_a)
    for _ in range(RW_CHUNK.bit_length() - 1):
        nt = _bmm(nt[:, :, :RW_CHUNK].astype(BF16), nt.astype(BF16)) + jnp.where(right, nt, 0.0)
    t_b = nt.astype(BF16)
    w_m = _bmm(t_b, jnp.concatenate([at, at], axis=1)).astype(BF16)
    av = _bmm(a_ak, vv).astype(BF16)
    u0 = _bmm(t_b, jnp.concatenate([av, av], axis=1)).astype(BF16)
    uv = jnp.concatenate([u0, vv], axis=1)
    m_p = _bmm_tn(bt, w_m).astype(BF16)
    g_t = _bmm_tn(uv, bk)
    r_p = (rt + _bmm(a_rb, w_m)).astype(BF16)
    o_loc = _bmm(a_r, uv)

    s = s_ref[0]
    for c in range(n_chunks):
        tiles = slice(c * B_HEADS, (c + 1) * B_HEADS)
        sb = s.astype(BF16)
        o_c = _bmm_nt(r_p[tiles], sb) + o_loc[tiles]
        s = (s + _bmm_nt(sb, m_p[tiles]) + g_t[tiles]) * pc_ref[tiles]
        for h in range(B_HEADS):
            o_ref[c * RW_CHUNK:(c + 1) * RW_CHUNK, h * HEAD_DIM:(h + 1) * HEAD_DIM] = o_c[h]
    s_ref[0] = s

    y = o_ref[...]
    yc = y - _head_sum(y, bd) * (1.0 / HEAD_DIM)
    yn = yc * lax.rsqrt(_head_sum(yc * yc, bd) * (1.0 / HEAD_DIM) + GN_EPS) * lng_ref[...] + lnb_ref[...]
    y_ref[...] = ((yn + post_ref[0]) * post_ref[1]).astype(BF16)


def _split_dot_left(m, x):
    hi = x.astype(BF16)
    lo = (x - hi.astype(F32)).astype(BF16)
    return _dot(m, hi) + _dot(m, lo)


def _rwkv_prompt(pb, params, *, batch, seq, tt):
    m = pb.shape[0]
    nt = seq // tt
    in_specs = [pl.BlockSpec((tt, B_COLS), lambda b, t: (b * nt + t, 0))] + [_const_spec(p.shape) for p in params]
    n_tiles = B_HEADS * tt // RW_CHUNK
    head_scratch = lambda dt: pltpu.VMEM((n_tiles, RW_CHUNK, HEAD_DIM), dt)
    return pl.pallas_call(
        _rwkv_prompt_kernel,
        grid=(batch, nt),
        in_specs=in_specs,
        out_specs=[pl.BlockSpec((tt, B_WIDTH), lambda b, t: (b * nt + t, 0)),
                   pl.BlockSpec((1, B_HEADS, HEAD_DIM, HEAD_DIM), lambda b, t: (b, 0, 0, 0))],
        out_shape=[jax.ShapeDtypeStruct((m, B_WIDTH), BF16),
                   jax.ShapeDtypeStruct((batch, B_HEADS, HEAD_DIM, HEAD_DIM), F32)],
        scratch_shapes=[pltpu.VMEM((8, B_COLS), F32), head_scratch(BF16), head_scratch(F32), head_scratch(BF16),
                        head_scratch(BF16), head_scratch(BF16), pltpu.VMEM((n_tiles, 1, HEAD_DIM), F32),
                        pltpu.VMEM((tt, B_WIDTH), F32), pltpu.VMEM((2, tt, B_WIDTH), F32)],
        compiler_params=pltpu.CompilerParams(dimension_semantics=("arbitrary", "arbitrary"),
                                             vmem_limit_bytes=VMEM_LIMIT),
        name="rwkv_prompt",
    )(pb, *params)


def _rwkv_step_kernel(first_layer, pb_ref, sh_ref, s_ref, mu_ref, w0_ref, w2_ref, a0_ref, a2_ref, g2_ref, kk_ref,
                      ka_ref, rk_ref, lng_ref, lnb_ref, bd_ref, *rest):
    if first_layer:
        y_ref, so_ref, row_ref, col_ref, yt_ref = rest
    else:
        _, y_ref, so_ref, row_ref, col_ref, yt_ref = rest
    h = pl.program_id(0)
    bd = bd_ref[...]

    @pl.when(h == 0)
    def _():
        r, k2, v, kk, a, log_decay, g = _rwkv_prep(pb_ref[...], sh_ref[...], mu_ref[...], w0_ref[...], w2_ref[...],
                                                   a0_ref[...], a2_ref[...], g2_ref[...], kk_ref[...], ka_ref[...], bd)
        for n, x in enumerate((r, k2, v, g)):
            row_ref[n] = x
        for n, x in enumerate((r, jnp.exp(log_decay), k2, v, kk, kk * a)):
            col_ref[n] = x.T

    rows = pl.ds(pl.multiple_of(h * HEAD_DIM, HEAD_DIM), HEAD_DIM)
    r_c, w_c, k_c, v_c, kk_c, b_c = (col_ref[n, rows, :] for n in range(6))
    for i in range(HEAD_DIM):
        s = s_ref[0, 0, i]
        s_kk = jnp.sum(s * kk_c, axis=0, keepdims=True)
        s_new = s * w_c - s_kk * b_c + v_c[i:i + 1] * k_c
        so_ref[0, 0, i] = s_new
        yt_ref[pl.ds(h * HEAD_DIM + i, 1), :] = jnp.sum(s_new * r_c, axis=0, keepdims=True)

    @pl.when(h == pl.num_programs(0) - 1)
    def _():
        y_ref[...] = _rwkv_post(yt_ref[...].T, row_ref[0], row_ref[1], row_ref[2], row_ref[3], rk_ref[...],
                                lng_ref[...], lnb_ref[...], bd).astype(BF16)


def _rwkv_step(pb, shift, state_t, states_out, params, layer):
    m = pb.shape[0]
    st_spec = pl.BlockSpec((1, 1, HEAD_DIM, HEAD_DIM, m), lambda h: (layer, h, 0, 0, 0))
    full = lambda width: pl.BlockSpec((m, width), lambda h: (0, 0))
    in_specs = [full(B_COLS), pl.BlockSpec((None, m, B_COLS), lambda h: (layer, 0, 0)), st_spec]
    in_specs += [_const_spec(p.shape) for p in params]
    args = [pb, shift, state_t, *params]
    aliases = {}
    if states_out is not None:
        in_specs.append(pl.BlockSpec(memory_space=pl.ANY))
        aliases = {len(args): 1}
        args.append(states_out)
    return pl.pallas_call(
        functools.partial(_rwkv_step_kernel, states_out is None),
        grid=(B_HEADS,),
        in_specs=in_specs,
        out_specs=[full(B_WIDTH), st_spec],
        out_shape=[jax.ShapeDtypeStruct((m, B_WIDTH), BF16), jax.ShapeDtypeStruct(state_t.shape, F32)],
        input_output_aliases=aliases,
        scratch_shapes=[pltpu.VMEM((4, m, B_WIDTH), F32), pltpu.VMEM((6, B_WIDTH, m), F32),
                        pltpu.VMEM((B_WIDTH, m), F32)],
        compiler_params=pltpu.CompilerParams(dimension_semantics=("arbitrary",), vmem_limit_bytes=VMEM_LIMIT),
        name="rwkv_sample",
    )(*args)


def _sb_chains(z_chains, mask_chains, carries, tri):
    units = [(c, n) for c in range(len(z_chains)) for n in range(len(z_chains[c]))]
    log_beta, log_keep, after, att = {}, {}, {}, {}
    for c, n in units:
        z = z_chains[c][n]
        log_beta[c, n] = jnp.minimum(z, 0.0) - jnp.log(1.0 + jnp.exp(-jnp.abs(z)))
        keep = log_beta[c, n] - z
        log_keep[c, n] = keep if mask_chains[c][n] is None else jnp.where(mask_chains[c][n], keep, 0.0)
    for u in units:
        after[u] = _dot(log_keep[u].astype(BF16), tri)
    carries = list(carries)
    keys = tri.shape[0]
    for c, n in units:
        w = jnp.exp(log_beta[c, n] + after[c, n][:, :keys] + carries[c])
        att[c, n] = (w if mask_chains[c][n] is None else jnp.where(mask_chains[c][n], w, 0.0)).astype(BF16)
        carries[c] = carries[c] + after[c, n][:, keys:]
    return [[att[c, n] for n in range(len(z_chains[c]))] for c in range(len(z_chains))], carries


def _suffix_and_total(n):
    later = lax.broadcasted_iota(jnp.int32, (n, 2 * n), 0) > lax.broadcasted_iota(jnp.int32, (n, 2 * n), 1)
    whole = lax.broadcasted_iota(jnp.int32, (n, 2 * n), 1) >= n
    return jnp.where(later | whole, 1.0, 0.0).astype(BF16)


def _sb_prompt_kernel(bias_ref, q_ref, k_ref, v_ref, o_ref, acc_ref, carry_ref):
    blk = SB_BLOCK
    n_sub = q_ref.shape[1] // blk
    tri = _suffix_and_total(blk)
    causal = lax.broadcasted_iota(jnp.int32, (blk, blk), 1) < lax.broadcasted_iota(jnp.int32, (blk, blk), 0)

    def query_block(sub, _):
        i = pl.program_id(1) * n_sub + sub
        q_rows = pl.ds(pl.multiple_of(sub * blk, blk), blk)
        acc_ref[...] = jnp.zeros_like(acc_ref)
        carry_ref[...] = jnp.zeros_like(carry_ref)

        def process(blocks, masks):
            offs = [pl.multiple_of(j * blk, blk) for j in blocks]
            zs = [[_dot_nt(q_ref[h, q_rows, :], k_ref[h, pl.ds(off, blk), :]) + bias_ref[h] for off in offs]
                  for h in range(C_HEADS)]
            att, carry = _sb_chains(zs, [masks] * C_HEADS, [carry_ref[h] for h in range(C_HEADS)], tri)
            pv = [[_dot(att[h][n], v_ref[h, pl.ds(off, blk), :]) for n, off in enumerate(offs)]
                  for h in range(C_HEADS)]
            for h in range(C_HEADS):
                acc_ref[h] += functools.reduce(lambda a, b: a + b, pv[h])
                carry_ref[h] = carry[h]

        for rem in range(SB_GROUP):
            @pl.when(i % SB_GROUP == rem)
            def _():
                process([i - n for n in range(rem + 1)], [causal] + [None] * rem)

        def group(p, _):
            first = i - i % SB_GROUP - 1 - SB_GROUP * p
            process([first - n for n in range(SB_GROUP)], [None] * SB_GROUP)
            return 0

        lax.fori_loop(0, i // SB_GROUP, group, 0)
        for h in range(C_HEADS):
            o_ref[q_rows, h * HEAD_DIM:(h + 1) * HEAD_DIM] = acc_ref[h].astype(BF16)
        return 0

    lax.fori_loop(0, n_sub, query_block, 0)


def _sb_prompt(bias, qh, kh, vh, *, batch, seq, n_sub):
    m = qh.shape[1]
    steps = seq // (SB_BLOCK * n_sub)
    rows = SB_BLOCK * n_sub
    kv_spec = pl.BlockSpec((C_HEADS, seq, HEAD_DIM), lambda b, i: (0, b, 0))
    return pl.pallas_call(
        _sb_prompt_kernel,
        grid=(batch, steps),
        in_specs=[pl.BlockSpec(memory_space=pltpu.SMEM),
                  pl.BlockSpec((C_HEADS, rows, HEAD_DIM), lambda b, i: (0, b * steps + i, 0)), kv_spec, kv_spec],
        out_specs=pl.BlockSpec((rows, C_WIDTH), lambda b, i: (b * steps + i, 0)),
        out_shape=jax.ShapeDtypeStruct((m, C_WIDTH), BF16),
        scratch_shapes=[pltpu.VMEM((C_HEADS, SB_BLOCK, HEAD_DIM), F32), pltpu.VMEM((C_HEADS, SB_BLOCK, SB_BLOCK), F32)],
        compiler_params=pltpu.CompilerParams(dimension_semantics=("arbitrary", "arbitrary"),
                                             vmem_limit_bytes=VMEM_LIMIT),
        name="sb_prompt",
    )(bias, qh, kh, vh)


def _sb_sample_kernel(n_seq, n_pages, page_size, pt_ref, bias_ref, q_ref, kn_ref, vn_ref, *refs):
    n_blocks = n_seq * n_pages
    k_refs = refs[:n_blocks]
    v_refs = refs[n_blocks:2 * n_blocks]
    o_ref = refs[2 * n_blocks]
    rows = 8
    tri = _suffix_and_total(page_size)
    r_col = lax.broadcasted_iota(jnp.int32, (rows, 1), 0)
    bias = jnp.zeros((rows, 1), F32)
    for h in range(C_HEADS):
        bias = jnp.where(r_col == h, bias_ref[h], bias)
    head_lanes = (lax.broadcasted_iota(jnp.int32, (rows, C_WIDTH), 1) // HEAD_DIM
                  == lax.broadcasted_iota(jnp.int32, (rows, C_WIDTH), 0))
    past = n_pages * page_size
    new_is_causal = (past + 0) < (past + 0)
    pages = list(reversed(range(n_pages)))

    z_chains, carries, w_new = [], [], []
    for s in range(n_seq):
        q = jnp.where(head_lanes, q_ref[s], 0.0)
        q_b = q.astype(BF16)
        z_new = jnp.sum(q * kn_ref[s], axis=-1, keepdims=True) + bias
        carries.append(jnp.broadcast_to(jnp.where(new_is_causal, -_softplus(z_new), 0.0), (rows, page_size)))
        w_new.append(jnp.where(new_is_causal, jnp.exp(-_softplus(-z_new)), 0.0))
        z_chains.append([_dot(q_b, k_refs[s * n_pages + p][0, 0].astype(BF16)) + bias for p in pages])
    att, _ = _sb_chains(z_chains, [[None] * n_pages] * n_seq, carries, tri)
    for s in range(n_seq):
        pv = [_dot_nt(att[s][n], v_refs[s * n_pages + p][0, 0].astype(BF16)) for n, p in enumerate(pages)]
        out = w_new[s] * vn_ref[s] + functools.reduce(lambda a, b: a + b, pv)
        o_ref[s] = jnp.sum(jnp.where(head_lanes, out, 0.0), axis=0, keepdims=True).astype(BF16)


def _sb_sample(page_table, bias, q, k_new, v_new, cache_k, cache_v, layer, *, n_seq):
    m = q.shape[0]
    n_pages = page_table.shape[1]
    page_size = cache_k.shape[2]
    to_pages = lambda c: jnp.transpose(c, (0, 1, 3, 4, 2)).reshape(c.shape[0], c.shape[1], C_WIDTH, page_size)
    cache_k, cache_v = to_pages(cache_k), to_pages(cache_v)
    pt = page_table.reshape(-1)
    row3 = lambda x: x.reshape(m, 1, C_WIDTH)
    row_spec = pl.BlockSpec((n_seq, 1, C_WIDTH), lambda b, pt: (b, 0, 0))

    def page_spec(s, p):
        return pl.BlockSpec((1, 1, C_WIDTH, page_size),
                            lambda b, pt: (layer, pt[(b * n_seq + s) * n_pages + p], 0, 0))

    page_specs = [page_spec(s, p) for s in range(n_seq) for p in range(n_pages)]
    grid_spec = pltpu.PrefetchScalarGridSpec(
        num_scalar_prefetch=1,
        grid=(m // n_seq,),
        in_specs=[pl.BlockSpec(memory_space=pltpu.SMEM), row_spec, row_spec, row_spec] + page_specs + page_specs,
        out_specs=row_spec,
    )
    n_blocks = n_seq * n_pages
    out = pl.pallas_call(
        functools.partial(_sb_sample_kernel, n_seq, n_pages, page_size),
        grid_spec=grid_spec,
        out_shape=jax.ShapeDtypeStruct((m, 1, C_WIDTH), BF16),
        compiler_params=pltpu.CompilerParams(dimension_semantics=("arbitrary",), vmem_limit_bytes=VMEM_LIMIT),
        name="sb_sample",
    )(pt, bias, row3(q), row3(k_new), row3(v_new), *([cache_k] * n_blocks), *([cache_v] * n_blocks))
    return out.reshape(m, C_WIDTH)


def _tail_kernel(h_ref, ya_ref, yb_ref, yc_ref, pe_ref, wo_ref, fg_ref, wg_ref, wu_ref, wd_ref, pg_ref, wpg_ref,
                 wple_ref, out_ref):
    h = (h_ref[...] + _dot(ya_ref[...], wo_ref[0:A_WIDTH, :]) + _dot(yb_ref[...], wo_ref[A_WIDTH:A_WIDTH + B_WIDTH, :])
         + _dot(yc_ref[...], wo_ref[A_WIDTH + B_WIDTH:, :]))
    hn = _rms(h, fg_ref[...]).astype(BF16)
    gate = _dot(hn, wg_ref[...])
    act = (gate * jax.nn.sigmoid(gate) * _dot(hn, wu_ref[...])).astype(BF16)
    h = h + _dot(act, wd_ref[...])
    gate = jax.nn.sigmoid(_dot(_rms(h, pg_ref[...]).astype(BF16), wpg_ref[...]))
    out_ref[...] = h + _dot(pe_ref[...].astype(BF16), wple_ref[...]) * gate


def _tail(h, ya, yb, yc, pe, wo, fg, wg, wu, wd, pg, wpg, wple, *, tm, layer):
    m, d = h.shape
    row = lambda width: pl.BlockSpec((tm, width), lambda i: (i, 0))
    weights = (wo, fg, wg, wu, wd, pg, wpg, wple)
    return pl.pallas_call(
        _tail_kernel,
        grid=(m // tm,),
        in_specs=[row(d), row(A_WIDTH), row(B_WIDTH), row(C_WIDTH),
                  pl.BlockSpec((None, tm, pe.shape[2]), lambda i: (layer, i, 0))]
        + [_const_spec(w.shape) for w in weights],
        out_specs=row(d),
        out_shape=jax.ShapeDtypeStruct((m, d), F32),
        compiler_params=pltpu.CompilerParams(dimension_semantics=("arbitrary",), vmem_limit_bytes=VMEM_LIMIT),
        name="tail",
    )(h, ya, yb, yc, pe, *weights)


def _pad_rows(w, top, total):
    return jnp.pad(w, ((top, total - top - w.shape[0]), (0, 0)))


def kernel(x_prompt, x_sample, cache_k, cache_v, state_wkv, state_shift, page_table, p_prompt, p_sample, mix_norm, w_in, a_ln_g, a_ln_b, a_ws, a_bs, b_mu, b_w0, b_w2, b_a0, b_a2, b_g2, b_kk, b_ka, b_rk, b_ln_g, b_ln_b, c_qn, c_kn, c_bias, w_out, ffn_norm, w_gate, w_up, w_down, ple_norm, w_ple_gate, w_ple):
    depth = w_in.shape[0]
    bp, seq, d = x_prompt.shape
    bs = x_sample.shape[0]
    hp = x_prompt.reshape(bp * seq, d)
    hs = x_sample.reshape(bs, d)
    hd = lax.broadcasted_iota(jnp.int32, (B_WIDTH, B_WIDTH), 0) // HEAD_DIM
    bd = (hd == hd.T).astype(BF16)
    row = lambda x: x.reshape(1, -1)

    outs = {n: [] for n in ("wkvp", "shp", "ks", "vs", "shs", "avs")}
    pe_p = p_prompt.reshape(depth, bp * seq, -1)
    pe_s = p_sample.reshape(depth, bs, -1)
    state_t = jnp.transpose(state_wkv, (0, 2, 3, 4, 1))
    kv_shape = (depth, bp, C_HEADS, HEAD_DIM, seq)
    kv_bufs = None
    states_out = None
    for i in range(depth):
        w_in_b = w_in[i].astype(BF16)
        proj_common = (row(mix_norm[i]), w_in_b, row(a_ln_g[i]), row(a_ln_b[i]))
        bias_lanes = jnp.repeat(a_bs[i].T, HEAD_DIM, axis=1)
        w00_lanes = jnp.repeat(a_ws[i][:, 0, 0], HEAD_DIM).reshape(1, A_WIDTH)
        rw_params = (row(b_mu[i]), row(b_w0[i]), _pad_rows(b_w2[i], 0, LORA_WA).astype(BF16), row(b_a0[i]),
                     _pad_rows(b_a2[i], LORA_WA - b_a2.shape[1], LORA_WA).astype(BF16), b_g2[i].astype(BF16),
                     row(b_kk[i]), row(b_ka[i]), row(b_rk[i]), row(b_ln_g[i]), row(b_ln_b[i]), bd)
        tail_w = (w_out[i].astype(BF16), row(ffn_norm[i]), w_gate[i].astype(BF16), w_up[i].astype(BF16),
                  w_down[i].astype(BF16), row(ple_norm[i]), w_ple_gate[i].astype(BF16), w_ple[i].astype(BF16))
        qk_norm = (row(jnp.tile(c_qn[i], C_HEADS)), row(jnp.tile(c_kn[i], C_HEADS)), bd[:C_WIDTH, :C_WIDTH])

        ya, pb, qh, kh, vh, *kv_bufs = _proj(hp, *proj_common, a_ws[i], bias_lanes, *qk_norm, sample=False, tm=512,
                                             layer=i, seq=seq, kv_shape=kv_shape, kv_bufs=kv_bufs)
        yb, s_fin = _rwkv_prompt(pb, rw_params, batch=bp, seq=seq, tt=256)
        yc = _sb_prompt(c_bias[i], qh, kh, vh, batch=bp, seq=seq, n_sub=4)
        hp = _tail(hp, ya, yb, yc, pe_p, *tail_w, tm=512, layer=i)
        outs["wkvp"].append(s_fin)
        outs["shp"].append(pb.reshape(bp, seq, B_COLS)[:, -1])

        ya, pb, q_n, k_n, v_n, a_vn = _proj(hs, *proj_common, w00_lanes, bias_lanes[0:1], *qk_norm, sample=True, tm=bs)
        yb, states_out = _rwkv_step(pb, state_shift, state_t, states_out, rw_params, i)
        yc = _sb_sample(page_table, c_bias[i], q_n, k_n, v_n, cache_k, cache_v, i, n_seq=4)
        hs = _tail(hs, ya, yb, yc, pe_s, *tail_w, tm=bs, layer=i)
        outs["ks"].append(k_n.reshape(bs, 1, C_HEADS, HEAD_DIM))
        outs["vs"].append(v_n.reshape(bs, 1, C_HEADS, HEAD_DIM))
        outs["shs"].append(pb)
        outs["avs"].append(a_vn.reshape(bs, 1, A_WIDTH))

    st = {n: jnp.stack(v) for n, v in outs.items()}
    k_prompt, v_prompt = (jnp.transpose(x, (0, 1, 4, 2, 3)) for x in kv_bufs)
    wkv_sample = jnp.transpose(states_out, (0, 4, 1, 2, 3))
    return (hp.reshape(bp, seq, d), hs.reshape(bs, 1, d), k_prompt, v_prompt, st["wkvp"], st["shp"],
            st["ks"], st["vs"], wkv_sample, st["shs"], st["avs"])
```
